```python
import math
import jax, jax.numpy as jnp
from jax import lax
import numpy as np

D_MODEL = 2048
BATCH = 4
SEQ = 4096
DEPTH = 2

DILATED_GROUPS = ((128, 1), (512, 4), (2048, 16))
N_ATTN_GROUPS = len(DILATED_GROUPS)
HEADS_PER_GROUP = 4
HEAD_DIM = 128
N_ATTN_HEADS = N_ATTN_GROUPS * HEADS_PER_GROUP
QKV_W = N_ATTN_HEADS * HEAD_DIM
ATTN_OUT = HEADS_PER_GROUP * HEAD_DIM
REL_BUCKETS = 32
REL_MAX_DIST = 2048
D_INNER = 2 * D_MODEL
SSM_HEAD_DIM = 64
SSM_HEADS = D_INNER // SSM_HEAD_DIM
SSM_GROUPS = 8
HEADS_PER_SSM_GROUP = SSM_HEADS // SSM_GROUPS
D_STATE = 128
CONV_K = 4
CHUNK = 128
CONV_DIM = D_INNER + 2 * SSM_GROUPS * D_STATE
D_FF = ((8 * D_MODEL // 3 + 255) // 256) * 256
SPLITS = (QKV_W, QKV_W, QKV_W, D_INNER, CONV_DIM, SSM_HEADS, D_MODEL, D_MODEL)
N_IN = sum(SPLITS)
EPS = 1e-6

kernel_name = "hybrid_dilated_attn_ssd_gated_block"


def rmsnorm(x, w):
    x32 = x.astype(jnp.float32)
    y = x32 * lax.rsqrt(jnp.mean(x32 * x32, axis=-1, keepdims=True) + EPS)
    return (y * w.astype(jnp.float32)).astype(x.dtype)


def t5_bucket(dist):
    exact = REL_BUCKETS // 2
    n = jnp.maximum(dist, 1).astype(jnp.float32)
    large = exact + (jnp.log(n / exact) / math.log(REL_MAX_DIST / exact)
                     * (REL_BUCKETS - exact)).astype(jnp.int32)
    large = jnp.minimum(large, REL_BUCKETS - 1)
    return jnp.where(dist < exact, dist, large)


def dilated_window_attention(q, k, v, bias_table, window, dilation):
    B, S, Hg, Dh = q.shape
    blk = window // dilation
    unit = blk * dilation
    Sp = -(-S // unit) * unit
    L = Sp // dilation
    nb = L // blk

    def to_blocks(t):
        t = jnp.pad(t, ((0, 0), (0, Sp - S), (0, 0), (0, 0)))
        t = t.reshape(B, L, dilation, Hg, Dh).transpose(0, 2, 1, 3, 4)
        return t.reshape(B, dilation, nb, blk, Hg, Dh)

    qb, kb, vb = to_blocks(q), to_blocks(k), to_blocks(v)

    def with_prev(t):
        prev = jnp.pad(t[:, :, :-1], ((0, 0), (0, 0), (1, 0), (0, 0), (0, 0), (0, 0)))
        return jnp.concatenate([prev, t], axis=3)

    kk, vv = with_prev(kb), with_prev(vb)
    logits = jnp.einsum('brnqhd,brnkhd->brnhqk', qb, kk).astype(jnp.float32) / math.sqrt(Dh)

    qi = jnp.arange(blk)[:, None]
    kj = jnp.arange(2 * blk)[None, :]
    steps = blk + qi - kj
    band = (steps >= 0) & (steps <= blk)
    first = (jnp.arange(nb) == 0)[:, None, None]
    valid = band[None] & ~(first & (kj < blk)[None])
    bucket = t5_bucket(jnp.clip(steps, 0, blk) * dilation)
    bias = bias_table[bucket].astype(jnp.float32).transpose(2, 0, 1)

    logits = jnp.where(valid[None, None, :, None], logits + bias[None, None, None], -jnp.inf)
    m = jnp.max(logits, axis=-1, keepdims=True)
    p = jnp.exp(logits - m)
    s = jnp.sum(p, axis=-1, keepdims=True)
    o = jnp.einsum('brnhqk,brnkhd->brnqhd', (p / s).astype(v.dtype), vv)
    lse = (m + jnp.log(s))[..., 0]

    o = o.reshape(B, dilation, L, Hg, Dh).transpose(0, 2, 1, 3, 4).reshape(B, Sp, Hg, Dh)[:, :S]
    lse = lse.transpose(0, 1, 2, 4, 3).reshape(B, dilation, L, Hg)
    lse = lse.transpose(0, 2, 1, 3).reshape(B, Sp, Hg)[:, :S]
    return o, lse


def ssd_chunked(xs, dt, A, Bm, Cm):
    Bsz, S, H, P = xs.shape
    G, N = Bm.shape[2], Bm.shape[3]
    R = H // G
    nc = S // CHUNK
    f32 = jnp.float32
    x = xs.astype(f32).reshape(Bsz, nc, CHUNK, G, R, P)
    dtc = dt.astype(f32).reshape(Bsz, nc, CHUNK, G, R)
    Bc = Bm.astype(f32).reshape(Bsz, nc, CHUNK, G, N)
    Cc = Cm.astype(f32).reshape(Bsz, nc, CHUNK, G, N)
    acum = jnp.cumsum(dtc * A.astype(f32).reshape(G, R), axis=2)
    xdt = x * dtc[..., None]

    at = jnp.moveaxis(acum, 2, -1)
    seg = at[..., :, None] - at[..., None, :]
    tril = jnp.tril(jnp.ones((CHUNK, CHUNK), dtype=bool))
    Ldec = jnp.exp(jnp.where(tril, seg, -jnp.inf))
    cb = jnp.einsum('bclgn,bcsgn->bcgls', Cc, Bc)
    y_diag = jnp.einsum('bcgrls,bcsgrp->bclgrp', cb[:, :, :, None] * Ldec, xdt)

    decay_states = jnp.exp(acum[:, :, -1:] - acum)
    states = jnp.einsum('bclgn,bclgrp->bcgrpn', Bc, xdt * decay_states[..., None])
    chunk_decay = jnp.exp(acum[:, :, -1])

    def step(h, inp):
        st, dec = inp
        return h * dec[..., None, None] + st, h

    h0 = jnp.zeros_like(states[:, 0])
    _, prev = lax.scan(step, h0, (jnp.moveaxis(states, 1, 0), jnp.moveaxis(chunk_decay, 1, 0)))
    prev = jnp.moveaxis(prev, 0, 1)

    y_off = jnp.einsum('bclgn,bcgrpn,bclgr->bclgrp', Cc, prev, jnp.exp(acum))
    return (y_diag + y_off).reshape(Bsz, S, H, P)


def causal_depthwise_conv(u, w, b):
    S = u.shape[1]
    up = jnp.pad(u, ((0, 0), (CONV_K - 1, 0), (0, 0)))
    out = b
    for j in range(CONV_K):
        out = out + up[:, j:j + S] * w[j]
    return out


def mixer(h, rel_bias, w_in, conv_w, conv_b, dt_bias, a_log, d_skip, ssm_norm_w,
          w_attn_proj, w_ssm_proj, w_out):
    B, S, _ = h.shape
    proj = h @ w_in
    q, k, v, z, xbc, dt_raw, g_attn, g_ssm = jnp.split(
        proj, np.cumsum(SPLITS)[:-1].tolist(), axis=-1)

    q = q.reshape(B, S, N_ATTN_GROUPS, HEADS_PER_GROUP, HEAD_DIM)
    k = k.reshape(B, S, N_ATTN_GROUPS, HEADS_PER_GROUP, HEAD_DIM)
    v = v.reshape(B, S, N_ATTN_GROUPS, HEADS_PER_GROUP, HEAD_DIM)
    outs, lses = [], []
    for g, (window, dilation) in enumerate(DILATED_GROUPS):
        o, lse = dilated_window_attention(
            q[:, :, g], k[:, :, g], v[:, :, g],
            rel_bias[:, g * HEADS_PER_GROUP:(g + 1) * HEADS_PER_GROUP], window, dilation)
        outs.append(o)
        lses.append(lse)
    wts = jax.nn.softmax(jnp.stack(lses, axis=0), axis=0)
    attn = jnp.sum(wts[..., None] * jnp.stack(outs, axis=0).astype(jnp.float32), axis=0)
    attn = attn.astype(h.dtype).reshape(B, S, ATTN_OUT)

    xbc = jax.nn.silu(causal_depthwise_conv(xbc, conv_w, conv_b))
    xs, Bm, Cm = jnp.split(xbc, [D_INNER, D_INNER + SSM_GROUPS * D_STATE], axis=-1)
    xs = xs.reshape(B, S, SSM_HEADS, SSM_HEAD_DIM)
    dt = jax.nn.softplus(dt_raw.astype(jnp.float32) + dt_bias.astype(jnp.float32))
    A = -jnp.exp(a_log.astype(jnp.float32))
    y = ssd_chunked(xs, dt, A, Bm.reshape(B, S, SSM_GROUPS, D_STATE),
                    Cm.reshape(B, S, SSM_GROUPS, D_STATE))
    y = y + d_skip.astype(jnp.float32)[:, None] * xs.astype(jnp.float32)
    y = y.reshape(B, S, D_INNER) * jax.nn.silu(z.astype(jnp.float32))
    y = y.reshape(B, S, SSM_GROUPS, D_INNER // SSM_GROUPS)
    y = y * lax.rsqrt(jnp.mean(y * y, axis=-1, keepdims=True) + EPS)
    y = (y.reshape(B, S, D_INNER) * ssm_norm_w.astype(jnp.float32)).astype(h.dtype)

    merged = jax.nn.sigmoid(g_attn) * (attn @ w_attn_proj) + jax.nn.sigmoid(g_ssm) * (y @ w_ssm_proj)
    return merged @ w_out


def swiglu(h, w_ffn_in, w_ffn_out):
    hg, hu = jnp.split(h @ w_ffn_in, 2, axis=-1)
    return (jax.nn.silu(hg) * hu) @ w_ffn_out


def setup_inputs(seed: int = 0) -> dict:
    key = jax.random.key(seed)
    ks = jax.random.split(key, 24)
    f32 = jnp.float32

    def nrm(k, shape, s):
        return jax.random.normal(k, shape, f32) * s

    dt0 = jnp.exp(jax.random.uniform(ks[10], (DEPTH, SSM_HEADS), f32)
                  * (math.log(0.1) - math.log(0.001)) + math.log(0.001))
    return {
        "x": nrm(ks[0], (BATCH, SEQ, D_MODEL), 1.0),
        "c": nrm(ks[1], (BATCH, D_MODEL), 1.0),
        "rel_bias": nrm(ks[2], (REL_BUCKETS, N_ATTN_HEADS), 0.5),
        "norm1_w": 1.0 + nrm(ks[3], (DEPTH, D_MODEL), 0.05),
        "norm2_w": 1.0 + nrm(ks[4], (DEPTH, D_MODEL), 0.05),
        "w_mod": nrm(ks[5], (DEPTH, D_MODEL, 6 * D_MODEL), D_MODEL ** -0.5),
        "b_mod": nrm(ks[6], (DEPTH, 6 * D_MODEL), 0.01),
        "w_in": nrm(ks[7], (DEPTH, D_MODEL, N_IN), D_MODEL ** -0.5),
        "conv_w": nrm(ks[8], (DEPTH, CONV_K, CONV_DIM), CONV_K ** -0.5),
        "conv_b": nrm(ks[9], (DEPTH, CONV_DIM), 0.01),
        "dt_bias": dt0 + jnp.log(-jnp.expm1(-dt0)),
        "a_log": jnp.log(jax.random.uniform(ks[11], (DEPTH, SSM_HEADS), f32, 1.0, 16.0)),
        "d_skip": 1.0 + nrm(ks[12], (DEPTH, SSM_HEADS), 0.1),
        "ssm_norm_w": 1.0 + nrm(ks[13], (DEPTH, D_INNER), 0.05),
        "w_attn_proj": nrm(ks[14], (DEPTH, ATTN_OUT, D_MODEL), ATTN_OUT ** -0.5),
        "w_ssm_proj": nrm(ks[15], (DEPTH, D_INNER, D_MODEL), D_INNER ** -0.5),
        "w_out": nrm(ks[16], (DEPTH, D_MODEL, D_MODEL), D_MODEL ** -0.5),
        "w_ffn_in": nrm(ks[17], (DEPTH, D_MODEL, 2 * D_FF), D_MODEL ** -0.5),
        "w_ffn_out": nrm(ks[18], (DEPTH, D_FF, D_MODEL), D_FF ** -0.5),
        "final_norm_w": 1.0 + nrm(ks[19], (D_MODEL,), 0.05),
    }


def reference(x, c, rel_bias, norm1_w, norm2_w, w_mod, b_mod, w_in, conv_w, conv_b,
              dt_bias, a_log, d_skip, ssm_norm_w, w_attn_proj, w_ssm_proj, w_out,
              w_ffn_in, w_ffn_out, final_norm_w):
    c_act = jax.nn.silu(c)
    for l in range(DEPTH):
        mod = (c_act @ w_mod[l] + b_mod[l])[:, None, :]
        sh1, sc1, g1, sh2, sc2, g2 = jnp.split(mod, 6, axis=-1)
        h = rmsnorm(x, norm1_w[l]) * (1.0 + sc1) + sh1
        x = x + g1 * mixer(h, rel_bias, w_in[l], conv_w[l], conv_b[l], dt_bias[l], a_log[l],
                           d_skip[l], ssm_norm_w[l], w_attn_proj[l], w_ssm_proj[l], w_out[l])
        h = rmsnorm(x, norm2_w[l]) * (1.0 + sc2) + sh2
        x = x + g2 * swiglu(h, w_ffn_in[l], w_ffn_out[l])
    return rmsnorm(x, final_norm_w)
```

```python
import functools
import math

import jax
import jax.numpy as jnp
import numpy as np
from jax import lax
from jax.experimental import pallas as pl
from jax.experimental.pallas import tpu as pltpu

F32 = jnp.float32
BF16 = jnp.bfloat16

D_MODEL = 2048
BATCH = 4
SEQ = 4096
TOKENS = BATCH * SEQ
DEPTH = 2
DILATED_GROUPS = ((128, 1), (512, 4), (2048, 16))
N_ATTN_GROUPS = len(DILATED_GROUPS)
HEADS_PER_GROUP = 4
HEAD_DIM = 128
QKV_W = N_ATTN_GROUPS * HEADS_PER_GROUP * HEAD_DIM
ATTN_OUT = HEADS_PER_GROUP * HEAD_DIM
ATTN_BLK = 128
REL_BUCKETS = 32
REL_MAX_DIST = 2048
D_INNER = 2 * D_MODEL
SSM_HEAD_DIM = 64
SSM_HEADS = D_INNER // SSM_HEAD_DIM
SSM_GROUPS = 8
HEADS_PER_SSM_GROUP = SSM_HEADS // SSM_GROUPS
GROUP_W = HEADS_PER_SSM_GROUP * SSM_HEAD_DIM
D_STATE = 128
CONV_K = 4
CHUNK = 128
CONV_DIM = D_INNER + 2 * SSM_GROUPS * D_STATE
CONV_GROUP_W = GROUP_W + 2 * D_STATE
D_FF = 5632
SPLITS = (QKV_W, QKV_W, QKV_W, D_INNER, CONV_DIM, SSM_HEADS, D_MODEL, D_MODEL)
OFF_Q, OFF_K, OFF_V, OFF_Z, OFF_XBC, OFF_DT, OFF_GA, OFF_GS = np.cumsum((0,) + SPLITS[:-1]).tolist()
PROJ_W = sum(SPLITS) - SSM_HEADS
P_Q, P_K, P_V, P_Z, P_XBC = OFF_Q, OFF_K, OFF_V, OFF_Z, OFF_XBC
P_GA, P_GS = OFF_GA - SSM_HEADS, OFF_GS - SSM_HEADS
P_B = P_XBC + D_INNER
P_C = P_B + SSM_GROUPS * D_STATE
EPS = 1e-6

LANES = 128
VMEM_LIMIT_CAP = 60 * 1024 * 1024


def _nbytes(shape, dtype):
    return int(np.prod(shape)) * jnp.dtype(dtype).itemsize


def _params(sem, blocks, extra=0):
    need = 2 * sum(_nbytes(s, d) for s, d in blocks) + extra
    return pltpu.CompilerParams(dimension_semantics=sem,
                                vmem_limit_bytes=min(VMEM_LIMIT_CAP, need + (8 << 20)))


def _silu(v):
    return v / (1.0 + jnp.exp(-v))


def _sigmoid(v):
    return 1.0 / (1.0 + jnp.exp(-v))


MOD_TN = 1024
MOD_ROWS = 8


def _mod_kernel(c_ref, w_ref, b_ref, o_ref):
    ca = _silu(c_ref[...])
    o_ref[0] = jnp.dot(ca, w_ref[0], preferred_element_type=F32) + b_ref[0]


def _modulation(c, w_mod, b_mod):
    c8 = jnp.zeros((MOD_ROWS, D_MODEL), F32).at[:BATCH].set(c)
    n = 6 * D_MODEL
    blocks = [((MOD_ROWS, D_MODEL), F32), ((D_MODEL, MOD_TN), F32), ((1, MOD_TN), F32), ((MOD_ROWS, MOD_TN), F32)]
    return pl.pallas_call(
        _mod_kernel,
        grid=(DEPTH, n // MOD_TN),
        in_specs=[pl.BlockSpec((MOD_ROWS, D_MODEL), lambda l, j: (0, 0)),
                  pl.BlockSpec((1, D_MODEL, MOD_TN), lambda l, j: (l, 0, j)),
                  pl.BlockSpec((1, 1, MOD_TN), lambda l, j: (l, 0, j))],
        out_specs=pl.BlockSpec((1, MOD_ROWS, MOD_TN), lambda l, j: (l, 0, j)),
        out_shape=jax.ShapeDtypeStruct((DEPTH, MOD_ROWS, n), F32),
        compiler_params=_params(("arbitrary", "arbitrary"), blocks),
        name="modulation",
    )(c8, w_mod, b_mod.reshape(DEPTH, 1, n))


NORM_TS = 512


def _norm_kernel(x_ref, w_ref, sc_ref, sh_ref, o_ref, *, modulate):
    x = x_ref[0]
    y = x * lax.rsqrt(jnp.mean(x * x, axis=-1, keepdims=True) + EPS)
    y = y * w_ref[...]
    if modulate:
        y = y * (1.0 + sc_ref[0]) + sh_ref[0]
    o_ref[0] = y.astype(o_ref.dtype)


def _norm(x, w, sc, sh, out_dtype, modulate):
    blocks = [((NORM_TS, D_MODEL), F32), ((NORM_TS, D_MODEL), out_dtype)]
    return pl.pallas_call(
        functools.partial(_norm_kernel, modulate=modulate),
        grid=(BATCH, SEQ // NORM_TS),
        in_specs=[pl.BlockSpec((1, NORM_TS, D_MODEL), lambda b, i: (b, i, 0)),
                  pl.BlockSpec((1, D_MODEL), lambda b, i: (0, 0)),
                  pl.BlockSpec((1, 1, D_MODEL), lambda b, i: (b, 0, 0)),
                  pl.BlockSpec((1, 1, D_MODEL), lambda b, i: (b, 0, 0))],
        out_specs=pl.BlockSpec((1, NORM_TS, D_MODEL), lambda b, i: (b, i, 0)),
        out_shape=jax.ShapeDtypeStruct((BATCH, SEQ, D_MODEL), out_dtype),
        compiler_params=_params(("arbitrary", "arbitrary"), blocks, extra=3 * _nbytes((NORM_TS, D_MODEL), F32)),
        name="rmsnorm_mod" if modulate else "rmsnorm",
    )(x, w.reshape(1, D_MODEL), sc, sh)


MM_TM = 1024
MM_TN = 512


def _mm_kernel(a_ref, w_ref, o_ref):
    o_ref[...] = jnp.dot(a_ref[...], w_ref[...], preferred_element_type=F32).astype(o_ref.dtype)


def _matmul(a, w, out_dtype, name):
    m, k = a.shape
    n = w.shape[1]
    blocks = [((MM_TM, k), BF16), ((k, MM_TN), BF16), ((MM_TM, MM_TN), out_dtype)]
    return pl.pallas_call(
        _mm_kernel,
        grid=(m // MM_TM, n // MM_TN),
        in_specs=[pl.BlockSpec((MM_TM, k), lambda i, j: (i, 0)),
                  pl.BlockSpec((k, MM_TN), lambda i, j: (0, j))],
        out_specs=pl.BlockSpec((MM_TM, MM_TN), lambda i, j: (i, j)),
        out_shape=jax.ShapeDtypeStruct((m, n), out_dtype),
        compiler_params=_params(("arbitrary", "arbitrary"), blocks, extra=_nbytes((MM_TM, MM_TN), F32)),
        name=name,
    )(a, w)


DT_TS = 1024


def _dt_kernel(h_ref, wt_ref, bias_ref, o_ref):
    raw = lax.dot_general(wt_ref[...], h_ref[0], (((1,), (1,)), ((), ())), preferred_element_type=F32)
    v = raw + bias_ref[...]
    o_ref[0] = jnp.maximum(v, 0.0) + jnp.log1p(jnp.exp(-jnp.abs(v)))


def _dt_proj(h, w_dt_t, dt_bias):
    blocks = [((DT_TS, D_MODEL), BF16), ((SSM_HEADS, D_MODEL), BF16), ((SSM_HEADS, DT_TS), F32)]
    return pl.pallas_call(
        _dt_kernel,
        grid=(BATCH, SEQ // DT_TS),
        in_specs=[pl.BlockSpec((1, DT_TS, D_MODEL), lambda b, i: (b, i, 0)),
                  pl.BlockSpec((SSM_HEADS, D_MODEL), lambda b, i: (0, 0)),
                  pl.BlockSpec((SSM_HEADS, 1), lambda b, i: (0, 0))],
        out_specs=pl.BlockSpec((1, SSM_HEADS, DT_TS), lambda b, i: (b, 0, i)),
        out_shape=jax.ShapeDtypeStruct((BATCH, SSM_HEADS, SEQ), F32),
        compiler_params=_params(("arbitrary", "arbitrary"), blocks),
        name="dt_proj",
    )(h, w_dt_t, dt_bias.reshape(SSM_HEADS, 1))


PROJ_BLOCKS = PROJ_W // ATTN_OUT


def _attn_kernel(q_ref, kc_ref, kp_ref, vc_ref, vp_ref, bias_ref, o_ref, l_ref):
    n = pl.program_id(2)
    qi = lax.broadcasted_iota(jnp.int32, (ATTN_BLK, ATTN_BLK), 0)
    kj = lax.broadcasted_iota(jnp.int32, (ATTN_BLK, ATTN_BLK), 1)
    cur_ok = kj <= qi
    prev_ok = (kj >= qi) & (n > 0)
    scale = 1.0 / math.sqrt(HEAD_DIM)
    nt = (((1,), (1,)), ((), ()))
    for h in range(HEADS_PER_GROUP):
        cols = slice(h * HEAD_DIM, (h + 1) * HEAD_DIM)
        q = q_ref[0, :, cols]
        s_p = lax.dot_general(q, kp_ref[0, :, cols], nt, preferred_element_type=F32) * scale
        s_c = lax.dot_general(q, kc_ref[0, :, cols], nt, preferred_element_type=F32) * scale
        s_p = jnp.where(prev_ok, s_p + bias_ref[h, :, :ATTN_BLK], -jnp.inf)
        s_c = jnp.where(cur_ok, s_c + bias_ref[h, :, ATTN_BLK:], -jnp.inf)
        m = jnp.maximum(jnp.max(s_p, axis=-1, keepdims=True), jnp.max(s_c, axis=-1, keepdims=True))
        p_p = jnp.exp(s_p - m)
        p_c = jnp.exp(s_c - m)
        den = jnp.sum(p_p, axis=-1, keepdims=True) + jnp.sum(p_c, axis=-1, keepdims=True)
        acc = jnp.dot(p_p.astype(BF16), vp_ref[0, :, cols], preferred_element_type=F32)
        acc = acc + jnp.dot(p_c.astype(BF16), vc_ref[0, :, cols], preferred_element_type=F32)
        o_ref[0, :, cols] = acc / den
        l_ref[0, :, cols] = jnp.broadcast_to(m + jnp.log(den), (ATTN_BLK, HEAD_DIM))


def _attention_group(proj, bias, g):
    dil = DILATED_GROUPS[g][1]
    rows = SEQ // dil
    nb = rows // ATTN_BLK
    view = proj.reshape(BATCH, rows, dil * PROJ_W)

    def spec(off, prev):
        blk = off // ATTN_OUT + g
        if prev:
            return pl.BlockSpec((1, ATTN_BLK, ATTN_OUT), lambda b, r, n: (b, jnp.maximum(n - 1, 0), r * PROJ_BLOCKS + blk))
        return pl.BlockSpec((1, ATTN_BLK, ATTN_OUT), lambda b, r, n: (b, n, r * PROJ_BLOCKS + blk))

    out_spec = pl.BlockSpec((1, ATTN_BLK, ATTN_OUT), lambda b, r, n: (b, n, r))
    out_sds = jax.ShapeDtypeStruct((BATCH, rows, dil * ATTN_OUT), F32)
    blocks = [((ATTN_BLK, ATTN_OUT), BF16)] * 5 + [((HEADS_PER_GROUP, ATTN_BLK, 2 * ATTN_BLK), F32)] \
        + [((ATTN_BLK, ATTN_OUT), F32)] * 2
    o, lse = pl.pallas_call(
        _attn_kernel,
        grid=(BATCH, dil, nb),
        in_specs=[spec(P_Q, False), spec(P_K, False), spec(P_K, True), spec(P_V, False), spec(P_V, True),
                  pl.BlockSpec((HEADS_PER_GROUP, ATTN_BLK, 2 * ATTN_BLK), lambda b, r, n: (g, 0, 0))],
        out_specs=[out_spec, out_spec],
        out_shape=[out_sds, out_sds],
        compiler_params=_params(("arbitrary", "arbitrary", "arbitrary"), blocks),
        name=f"dilated_attn_g{g}",
    )(view, view, view, view, view, bias)
    return o.reshape(TOKENS, ATTN_OUT), lse.reshape(TOKENS, ATTN_OUT)


def _t5_bucket(dist):
    exact = REL_BUCKETS // 2
    n = jnp.maximum(dist, 1).astype(F32)
    large = exact + (jnp.log(n / exact) / math.log(REL_MAX_DIST / exact) * (REL_BUCKETS - exact)).astype(jnp.int32)
    large = jnp.minimum(large, REL_BUCKETS - 1)
    return jnp.where(dist < exact, dist, large)


def _attn_bias(rel_bias):
    qi = jnp.arange(ATTN_BLK)[:, None]
    kj = jnp.arange(2 * ATTN_BLK)[None, :]
    steps = jnp.clip(ATTN_BLK + qi - kj, 0, ATTN_BLK)
    out = []
    for g, (_, dil) in enumerate(DILATED_GROUPS):
        tab = rel_bias[:, g * HEADS_PER_GROUP:(g + 1) * HEADS_PER_GROUP]
        out.append(tab[_t5_bucket(steps * dil)].transpose(2, 0, 1))
    return jnp.concatenate(out, axis=0).astype(F32)


N_CHUNKS = SEQ // CHUNK
HIST = 8


def _split3(v):
    hi = v.astype(BF16)
    r1 = v - hi.astype(F32)
    mid = r1.astype(BF16)
    lo = (r1 - mid.astype(F32)).astype(BF16)
    return hi, mid, lo


def _ssd_kernel(xs_ref, bm_ref, cm_ref, z_ref, dtt_ref, cw_ref, cb_ref, acol_ref, dskip_ref, nw_ref,
                y_ref, state_ref, ext_ref):
    state_ref[...] = jnp.zeros_like(state_ref)
    ext_ref[0:HIST, :] = jnp.zeros((HIST, CONV_GROUP_W), F32)

    li = lax.broadcasted_iota(jnp.int32, (CHUNK, CHUNK), 0)
    si = lax.broadcasted_iota(jnp.int32, (CHUNK, CHUNK), 1)
    tril = li >= si
    ltri = jnp.where(tril, 1.0, 0.0).astype(BF16)
    utri = jnp.where(li <= si, 1.0, 0.0).astype(BF16)
    eye = jnp.where(li == si, 1.0, 0.0).astype(BF16)
    low_half = si < SSM_HEAD_DIM
    nt = (((1,), (1,)), ((), ()))

    cw = cw_ref[0]
    cb = cb_ref[0]
    a_col = -jnp.exp(acol_ref[0])
    dskip = dskip_ref[0]
    nw = nw_ref[...]

    def to_cols(vt):
        return sum(lax.dot_general(eye, p, nt, preferred_element_type=F32) for p in _split3(vt))

    def expand(v):
        blocks = []
        for k in range(HEADS_PER_SSM_GROUP // 2):
            a = jnp.broadcast_to(v[:, 2 * k:2 * k + 1], (CHUNK, LANES))
            b = jnp.broadcast_to(v[:, 2 * k + 1:2 * k + 2], (CHUNK, LANES))
            blocks.append(jnp.where(low_half, a, b))
        return jnp.concatenate(blocks, axis=1)

    def chunk(c, carry):
        r0 = pl.multiple_of(c * CHUNK, CHUNK)
        rows = pl.ds(r0, CHUNK)
        ext_ref[HIST:, 0:GROUP_W] = xs_ref[0, rows, :].astype(F32)
        ext_ref[HIST:, GROUP_W:GROUP_W + D_STATE] = bm_ref[0, rows, :].astype(F32)
        ext_ref[HIST:, GROUP_W + D_STATE:] = cm_ref[0, rows, :].astype(F32)
        acc = cb + cw[CONV_K - 1:CONV_K, :] * ext_ref[HIST:, :]
        for s in range(1, CONV_K):
            acc = acc + cw[CONV_K - 1 - s:CONV_K - s, :] * ext_ref[HIST - s:HIST - s + CHUNK, :]
        ext_ref[0:HIST, :] = ext_ref[CHUNK:CHUNK + HIST, :]
        xbc = _silu(acc)
        x = xbc[:, :GROUP_W]
        bmat = xbc[:, GROUP_W:GROUP_W + D_STATE]
        cmat = xbc[:, GROUP_W + D_STATE:].astype(BF16)

        dtt = dtt_ref[0, :, rows]
        dat = dtt * a_col
        dat3 = _split3(dat)
        acum_t = sum(jnp.dot(p, utri, preferred_element_type=F32) for p in dat3)
        acum = sum(lax.dot_general(ltri, p, nt, preferred_element_type=F32) for p in dat3)
        dt = to_cols(dtt)
        expa = jnp.exp(acum)
        decay = jnp.exp(acum[CHUNK - 1:CHUNK, :] - acum)
        dt_x = expand(dt)
        expa_x = expand(expa)
        decay_x = expand(decay)

        bt = bmat.T.astype(BF16)
        cbm = jnp.dot(cmat, bt, preferred_element_type=F32)
        xdt = x * dt_x
        xdt_b = xdt.astype(BF16)

        y_blocks = []
        for k in range(HEADS_PER_SSM_GROUP // 2):
            xk = xdt_b[:, k * LANES:(k + 1) * LANES]
            ms = []
            for r in (2 * k, 2 * k + 1):
                col = jnp.broadcast_to(acum[:, r:r + 1], (CHUNK, CHUNK))
                row = jnp.broadcast_to(acum_t[r:r + 1, :], (CHUNK, CHUNK))
                ldec = jnp.exp(jnp.where(tril, col - row, -jnp.inf))
                ms.append((cbm * ldec).astype(BF16))
            lhs = jnp.concatenate(ms, axis=1)
            rhs = jnp.concatenate([jnp.where(low_half, xk, jnp.zeros_like(xk)),
                                   jnp.where(low_half, jnp.zeros_like(xk), xk)], axis=0)
            y_blocks.append(jnp.dot(lhs, rhs, preferred_element_type=F32))
        y = jnp.concatenate(y_blocks, axis=1)

        state = state_ref[...]
        y = y + jnp.dot(cmat, state.astype(BF16), preferred_element_type=F32) * expa_x
        xw = (xdt * decay_x).astype(BF16)
        state_ref[...] = state * expa_x[CHUNK - 1:CHUNK, :] + jnp.dot(bt, xw, preferred_element_type=F32)

        y = y + dskip * x
        y = y * _silu(z_ref[0, rows, :].astype(F32))
        y = y * lax.rsqrt(jnp.mean(y * y, axis=-1, keepdims=True) + EPS)
        y_ref[0, rows, :] = (y * nw).astype(y_ref.dtype)
        return carry

    lax.fori_loop(0, N_CHUNKS, chunk, 0)


def _ssd(proj, dtt, conv_w_g, conv_b_g, a_col, dskip_x, norm_w):
    view = proj.reshape(BATCH, SEQ, PROJ_W)
    gw, ns = GROUP_W, D_STATE
    blocks = [((SEQ, gw), BF16)] * 3 + [((SEQ, ns), BF16)] * 2 + [((HEADS_PER_SSM_GROUP, SEQ), F32)]
    scratch_bytes = _nbytes((ns, gw), F32) + _nbytes((CHUNK + HIST, CONV_GROUP_W), F32)
    return pl.pallas_call(
        _ssd_kernel,
        grid=(BATCH, SSM_GROUPS),
        in_specs=[pl.BlockSpec((1, SEQ, gw), lambda b, g: (b, 0, P_XBC // gw + g)),
                  pl.BlockSpec((1, SEQ, ns), lambda b, g: (b, 0, P_B // ns + g)),
                  pl.BlockSpec((1, SEQ, ns), lambda b, g: (b, 0, P_C // ns + g)),
                  pl.BlockSpec((1, SEQ, gw), lambda b, g: (b, 0, P_Z // gw + g)),
                  pl.BlockSpec((1, HEADS_PER_SSM_GROUP, SEQ), lambda b, g: (b, g, 0)),
                  pl.BlockSpec((1, CONV_K, CONV_GROUP_W), lambda b, g: (g, 0, 0)),
                  pl.BlockSpec((1, 1, CONV_GROUP_W), lambda b, g: (g, 0, 0)),
                  pl.BlockSpec((1, HEADS_PER_SSM_GROUP, LANES), lambda b, g: (g, 0, 0)),
                  pl.BlockSpec((1, 1, gw), lambda b, g: (g, 0, 0)),
                  pl.BlockSpec((1, gw), lambda b, g: (0, g))],
        out_specs=pl.BlockSpec((1, SEQ, gw), lambda b, g: (b, 0, g)),
        out_shape=jax.ShapeDtypeStruct((BATCH, SEQ, D_INNER), BF16),
        scratch_shapes=[pltpu.VMEM((ns, gw), F32), pltpu.VMEM((CHUNK + HIST, CONV_GROUP_W), F32)],
        compiler_params=_params(("arbitrary", "arbitrary"), blocks, extra=scratch_bytes + (4 << 20)),
        name="ssd_scan",
    )(view, view, view, view, dtt, conv_w_g, conv_b_g, a_col, dskip_x, norm_w.reshape(1, D_INNER))


MG_TM = 512
MG_TN = 512


def _merge_kernel(o0_ref, o1_ref, o2_ref, l0_ref, l1_ref, l2_ref, y_ref, ga_ref, gs_ref, wa_ref, ws_ref,
                  out_ref, attn_ref):
    @pl.when(pl.program_id(1) == 0)
    def _():
        l0, l1, l2 = l0_ref[...], l1_ref[...], l2_ref[...]
        m = jnp.maximum(jnp.maximum(l0, l1), l2)
        e0, e1, e2 = jnp.exp(l0 - m), jnp.exp(l1 - m), jnp.exp(l2 - m)
        den = e0 + e1 + e2
        attn = (e0 / den) * o0_ref[...] + (e1 / den) * o1_ref[...] + (e2 / den) * o2_ref[...]
        attn_ref[...] = attn.astype(BF16)

    pa = jnp.dot(attn_ref[...], wa_ref[...], preferred_element_type=F32)
    ps = jnp.dot(y_ref[...], ws_ref[...], preferred_element_type=F32)
    merged = _sigmoid(ga_ref[...].astype(F32)) * pa + _sigmoid(gs_ref[...].astype(F32)) * ps
    out_ref[...] = merged.astype(out_ref.dtype)


def _merge_proj(outs, lses, y, proj, w_attn, w_ssm):
    row = lambda i, j: (i, 0)
    a_spec = pl.BlockSpec((MG_TM, ATTN_OUT), row)
    blocks = [((MG_TM, ATTN_OUT), F32)] * 6 + [((MG_TM, D_INNER), BF16)] + [((MG_TM, MG_TN), BF16)] * 3 \
        + [((ATTN_OUT, MG_TN), BF16), ((D_INNER, MG_TN), BF16)]
    return pl.pallas_call(
        _merge_kernel,
        grid=(TOKENS // MG_TM, D_MODEL // MG_TN),
        in_specs=[a_spec] * 6 + [
            pl.BlockSpec((MG_TM, D_INNER), row),
            pl.BlockSpec((MG_TM, MG_TN), lambda i, j: (i, P_GA // MG_TN + j)),
            pl.BlockSpec((MG_TM, MG_TN), lambda i, j: (i, P_GS // MG_TN + j)),
            pl.BlockSpec((ATTN_OUT, MG_TN), lambda i, j: (0, j)),
            pl.BlockSpec((D_INNER, MG_TN), lambda i, j: (0, j))],
        out_specs=pl.BlockSpec((MG_TM, MG_TN), lambda i, j: (i, j)),
        out_shape=jax.ShapeDtypeStruct((TOKENS, D_MODEL), BF16),
        scratch_shapes=[pltpu.VMEM((MG_TM, ATTN_OUT), BF16)],
        compiler_params=_params(("arbitrary", "arbitrary"), blocks, extra=4 * _nbytes((MG_TM, MG_TN), F32)),
        name="merge_proj",
    )(*outs, *lses, y, proj, proj, w_attn, w_ssm)


RS_TM = 512
RS_TN = 512


def _resid_kernel(a_ref, w_ref, x_ref, g_ref, o_ref):
    o_ref[...] = x_ref[...] + g_ref[0] * jnp.dot(a_ref[...], w_ref[...], preferred_element_type=F32)


def _resid_proj(a, w, x, gate, name):
    k = a.shape[1]
    blocks = [((RS_TM, k), BF16), ((k, RS_TN), BF16), ((RS_TM, RS_TN), F32), ((RS_TM, RS_TN), F32)]
    return pl.pallas_call(
        _resid_kernel,
        grid=(TOKENS // RS_TM, D_MODEL // RS_TN),
        in_specs=[pl.BlockSpec((RS_TM, k), lambda i, j: (i, 0)),
                  pl.BlockSpec((k, RS_TN), lambda i, j: (0, j)),
                  pl.BlockSpec((RS_TM, RS_TN), lambda i, j: (i, j)),
                  pl.BlockSpec((1, 1, RS_TN), lambda i, j: (i * RS_TM // SEQ, 0, j))],
        out_specs=pl.BlockSpec((RS_TM, RS_TN), lambda i, j: (i, j)),
        out_shape=jax.ShapeDtypeStruct((TOKENS, D_MODEL), F32),
        compiler_params=_params(("arbitrary", "arbitrary"), blocks, extra=_nbytes((RS_TM, RS_TN), F32)),
        name=name,
    )(a, w, x, gate)


FF_TM = 1024
FF_TN = 512


def _swiglu_kernel(h_ref, wg_ref, wu_ref, o_ref):
    h = h_ref[...]
    hg = jnp.dot(h, wg_ref[...], preferred_element_type=F32)
    hu = jnp.dot(h, wu_ref[...], preferred_element_type=F32)
    o_ref[...] = (_silu(hg) * hu).astype(o_ref.dtype)


def _swiglu_in(h, w):
    nblk = D_FF // FF_TN
    blocks = [((FF_TM, D_MODEL), BF16), ((D_MODEL, FF_TN), BF16), ((D_MODEL, FF_TN), BF16), ((FF_TM, FF_TN), BF16)]
    return pl.pallas_call(
        _swiglu_kernel,
        grid=(TOKENS // FF_TM, nblk),
        in_specs=[pl.BlockSpec((FF_TM, D_MODEL), lambda i, j: (i, 0)),
                  pl.BlockSpec((D_MODEL, FF_TN), lambda i, j: (0, j)),
                  pl.BlockSpec((D_MODEL, FF_TN), lambda i, j: (0, nblk + j))],
        out_specs=pl.BlockSpec((FF_TM, FF_TN), lambda i, j: (i, j)),
        out_shape=jax.ShapeDtypeStruct((TOKENS, D_FF), BF16),
        compiler_params=_params(("arbitrary", "arbitrary"), blocks, extra=3 * _nbytes((FF_TM, FF_TN), F32)),
        name="swiglu_in",
    )(h, w, w)


def _mixer(x, h, l, bias, w_in, conv_w, conv_b, dt_bias, a_log, d_skip, ssm_norm_w,
           w_attn_proj, w_ssm_proj, w_out, gate):
    w = w_in[l]
    w_big = jnp.concatenate([w[:, :OFF_DT], w[:, OFF_GA:]], axis=1).astype(BF16)
    w_dt_t = w[:, OFF_DT:OFF_GA].T.astype(BF16)
    h2d = h.reshape(TOKENS, D_MODEL)
    proj = _matmul(h2d, w_big, BF16, "in_proj")
    dtt = _dt_proj(h, w_dt_t, dt_bias[l])

    outs, lses = zip(*[_attention_group(proj, bias, g) for g in range(N_ATTN_GROUPS)])

    cw, cbias = conv_w[l], conv_b[l]
    xs_w = cw[:, :D_INNER].reshape(CONV_K, SSM_GROUPS, GROUP_W)
    b_w = cw[:, D_INNER:D_INNER + SSM_GROUPS * D_STATE].reshape(CONV_K, SSM_GROUPS, D_STATE)
    c_w = cw[:, D_INNER + SSM_GROUPS * D_STATE:].reshape(CONV_K, SSM_GROUPS, D_STATE)
    conv_w_g = jnp.concatenate([xs_w, b_w, c_w], axis=2).transpose(1, 0, 2)
    conv_b_g = jnp.concatenate([cbias[:D_INNER].reshape(SSM_GROUPS, GROUP_W),
                                cbias[D_INNER:D_INNER + SSM_GROUPS * D_STATE].reshape(SSM_GROUPS, D_STATE),
                                cbias[D_INNER + SSM_GROUPS * D_STATE:].reshape(SSM_GROUPS, D_STATE)],
                               axis=1).reshape(SSM_GROUPS, 1, CONV_GROUP_W)
    a_col = jnp.broadcast_to(a_log[l].reshape(SSM_GROUPS, HEADS_PER_SSM_GROUP, 1),
                             (SSM_GROUPS, HEADS_PER_SSM_GROUP, LANES))
    dskip_x = jnp.repeat(d_skip[l], SSM_HEAD_DIM).reshape(SSM_GROUPS, 1, GROUP_W)
    y = _ssd(proj, dtt, conv_w_g, conv_b_g, a_col, dskip_x, ssm_norm_w[l])

    merged = _merge_proj(outs, lses, y.reshape(TOKENS, D_INNER), proj,
                         w_attn_proj[l].astype(BF16), w_ssm_proj[l].astype(BF16))
    return _resid_proj(merged, w_out[l].astype(BF16), x, gate, "out_proj_resid")


def kernel(x, c, rel_bias, norm1_w, norm2_w, w_mod, b_mod, w_in, conv_w, conv_b, dt_bias, a_log, d_skip,
           ssm_norm_w, w_attn_proj, w_ssm_proj, w_out, w_ffn_in, w_ffn_out, final_norm_w):
    mod = _modulation(c, w_mod, b_mod)[:, :BATCH]
    bias = _attn_bias(rel_bias)
    xt = x.reshape(TOKENS, D_MODEL)
    for l in range(DEPTH):
        sh1, sc1, g1, sh2, sc2, g2 = [m.reshape(BATCH, 1, D_MODEL) for m in jnp.split(mod[l], 6, axis=-1)]
        h = _norm(xt.reshape(BATCH, SEQ, D_MODEL), norm1_w[l], sc1, sh1, BF16, True)
        xt = _mixer(xt, h, l, bias, w_in, conv_w, conv_b, dt_bias, a_log, d_skip, ssm_norm_w,
                    w_attn_proj, w_ssm_proj, w_out, g1)
        h = _norm(xt.reshape(BATCH, SEQ, D_MODEL), norm2_w[l], sc2, sh2, BF16, True)
        u = _swiglu_in(h.reshape(TOKENS, D_MODEL), w_ffn_in[l].astype(BF16))
        xt = _resid_proj(u, w_ffn_out[l].astype(BF16), xt, g2, "ffn_out_resid")
    zeros = jnp.zeros((BATCH, 1, D_MODEL), F32)
    return _norm(xt.reshape(BATCH, SEQ, D_MODEL), final_norm_w, zeros, zeros, F32, False)
```

```python
import functools
import math

import jax
import jax.numpy as jnp
import numpy as np
from jax import lax
from jax.experimental import pallas as pl
from jax.experimental.pallas import tpu as pltpu

F32 = jnp.float32
BF16 = jnp.bfloat16

D_MODEL = 2048
BATCH = 4
SEQ = 4096
TOKENS = BATCH * SEQ
DEPTH = 2
DILATED_GROUPS = ((128, 1), (512, 4), (2048, 16))
N_ATTN_GROUPS = len(DILATED_GROUPS)
HEADS_PER_GROUP = 4
HEAD_DIM = 128
QKV_W = N_ATTN_GROUPS * HEADS_PER_GROUP * HEAD_DIM
ATTN_OUT = HEADS_PER_GROUP * HEAD_DIM
ATTN_BLK = 128
REL_BUCKETS = 32
REL_MAX_DIST = 2048
D_INNER = 2 * D_MODEL
SSM_HEAD_DIM = 64
SSM_HEADS = D_INNER // SSM_HEAD_DIM
SSM_GROUPS = 8
HEADS_PER_SSM_GROUP = SSM_HEADS // SSM_GROUPS
GROUP_W = HEADS_PER_SSM_GROUP * SSM_HEAD_DIM
D_STATE = 128
CONV_K = 4
CHUNK = 128
CONV_DIM = D_INNER + 2 * SSM_GROUPS * D_STATE
CONV_GROUP_W = GROUP_W + 2 * D_STATE
D_FF = 5632
SPLITS = (QKV_W, QKV_W, QKV_W, D_INNER, CONV_DIM, SSM_HEADS, D_MODEL, D_MODEL)
OFF_Q, OFF_K, OFF_V, OFF_Z, OFF_XBC, OFF_DT, OFF_GA, OFF_GS = np.cumsum((0,) + SPLITS[:-1]).tolist()
N_QKV_HEADS = 3 * QKV_W // HEAD_DIM
PROJ_W = D_INNER + CONV_DIM + 2 * D_MODEL
P_Z = 0
P_XBC = D_INNER
P_B = P_XBC + D_INNER
P_C = P_B + SSM_GROUPS * D_STATE
P_GA = P_XBC + CONV_DIM
P_GS = P_GA + D_MODEL
EPS = 1e-6

LANES = 128
VMEM_LIMIT_CAP = 60 * 1024 * 1024


def _nbytes(shape, dtype):
    return int(np.prod(shape)) * jnp.dtype(dtype).itemsize


def _params(sem, blocks, extra=0):
    need = 2 * sum(_nbytes(s, d) for s, d in blocks) + extra
    return pltpu.CompilerParams(dimension_semantics=sem,
                                vmem_limit_bytes=min(VMEM_LIMIT_CAP, need + (8 << 20)))


def _silu(v):
    return v / (1.0 + jnp.exp(-v))


def _sigmoid(v):
    return 1.0 / (1.0 + jnp.exp(-v))


MOD_TN = 1024
MOD_ROWS = 8


def _mod_kernel(c_ref, w_ref, b_ref, o_ref):
    ca = _silu(c_ref[...])
    o_ref[0] = jnp.dot(ca, w_ref[0], preferred_element_type=F32) + b_ref[0]


def _modulation(c, w_mod, b_mod):
    c8 = jnp.zeros((MOD_ROWS, D_MODEL), F32).at[:BATCH].set(c)
    n = 6 * D_MODEL
    blocks = [((MOD_ROWS, D_MODEL), F32), ((D_MODEL, MOD_TN), F32), ((1, MOD_TN), F32), ((MOD_ROWS, MOD_TN), F32)]
    return pl.pallas_call(
        _mod_kernel,
        grid=(DEPTH, n // MOD_TN),
        in_specs=[pl.BlockSpec((MOD_ROWS, D_MODEL), lambda l, j: (0, 0)),
                  pl.BlockSpec((1, D_MODEL, MOD_TN), lambda l, j: (l, 0, j)),
                  pl.BlockSpec((1, 1, MOD_TN), lambda l, j: (l, 0, j))],
        out_specs=pl.BlockSpec((1, MOD_ROWS, MOD_TN), lambda l, j: (l, 0, j)),
        out_shape=jax.ShapeDtypeStruct((DEPTH, MOD_ROWS, n), F32),
        compiler_params=_params(("arbitrary", "arbitrary"), blocks),
        name="modulation",
    )(c8, w_mod, b_mod.reshape(DEPTH, 1, n))


NORM_TS = 512


def _norm_kernel(x_ref, w_ref, sc_ref, sh_ref, o_ref, *, modulate):
    x = x_ref[0]
    y = x * lax.rsqrt(jnp.mean(x * x, axis=-1, keepdims=True) + EPS)
    y = y * w_ref[...]
    if modulate:
        y = y * (1.0 + sc_ref[0]) + sh_ref[0]
    o_ref[0] = y.astype(o_ref.dtype)


def _norm(x, w, sc, sh, out_dtype, modulate):
    blocks = [((NORM_TS, D_MODEL), F32), ((NORM_TS, D_MODEL), out_dtype)]
    return pl.pallas_call(
        functools.partial(_norm_kernel, modulate=modulate),
        grid=(BATCH, SEQ // NORM_TS),
        in_specs=[pl.BlockSpec((1, NORM_TS, D_MODEL), lambda b, i: (b, i, 0)),
                  pl.BlockSpec((1, D_MODEL), lambda b, i: (0, 0)),
                  pl.BlockSpec((1, 1, D_MODEL), lambda b, i: (b, 0, 0)),
                  pl.BlockSpec((1, 1, D_MODEL), lambda b, i: (b, 0, 0))],
        out_specs=pl.BlockSpec((1, NORM_TS, D_MODEL), lambda b, i: (b, i, 0)),
        out_shape=jax.ShapeDtypeStruct((BATCH, SEQ, D_MODEL), out_dtype),
        compiler_params=_params(("arbitrary", "arbitrary"), blocks, extra=3 * _nbytes((NORM_TS, D_MODEL), F32)),
        name="rmsnorm_mod" if modulate else "rmsnorm",
    )(x, w.reshape(1, D_MODEL), sc, sh)


MM_TM = 1024
MM_TN = 1024


def _mm_kernel(a_ref, w_ref, o_ref):
    o_ref[...] = jnp.dot(a_ref[...], w_ref[...], preferred_element_type=F32).astype(o_ref.dtype)


def _matmul(a, w, out_dtype, name):
    m, k = a.shape
    n = w.shape[1]
    blocks = [((MM_TM, k), BF16), ((k, MM_TN), BF16), ((MM_TM, MM_TN), out_dtype)]
    return pl.pallas_call(
        _mm_kernel,
        grid=(m // MM_TM, n // MM_TN),
        in_specs=[pl.BlockSpec((MM_TM, k), lambda i, j: (i, 0)),
                  pl.BlockSpec((k, MM_TN), lambda i, j: (0, j))],
        out_specs=pl.BlockSpec((MM_TM, MM_TN), lambda i, j: (i, j)),
        out_shape=jax.ShapeDtypeStruct((m, n), out_dtype),
        compiler_params=_params(("arbitrary", "arbitrary"), blocks, extra=_nbytes((MM_TM, MM_TN), F32)),
        name=name,
    )(a, w)


QKV_TM = 2048
QKV_HEADS_PER_STEP = 4


def _qkv_kernel(h_ref, w_ref, o_ref):
    res = jnp.dot(h_ref[0], w_ref[...], preferred_element_type=F32)
    for hd in range(QKV_HEADS_PER_STEP):
        o_ref[0, hd] = res[:, hd * HEAD_DIM:(hd + 1) * HEAD_DIM]


def _qkv_proj(h, w):
    tn = QKV_HEADS_PER_STEP * HEAD_DIM
    blocks = [((QKV_TM, D_MODEL), BF16), ((D_MODEL, tn), BF16), ((QKV_TM, tn), F32)]
    return pl.pallas_call(
        _qkv_kernel,
        grid=(BATCH, SEQ // QKV_TM, N_QKV_HEADS // QKV_HEADS_PER_STEP),
        in_specs=[pl.BlockSpec((1, QKV_TM, D_MODEL), lambda b, i, j: (b, i, 0)),
                  pl.BlockSpec((D_MODEL, tn), lambda b, i, j: (0, j))],
        out_specs=pl.BlockSpec((1, QKV_HEADS_PER_STEP, QKV_TM, HEAD_DIM), lambda b, i, j: (b, j, i, 0)),
        out_shape=jax.ShapeDtypeStruct((BATCH, N_QKV_HEADS, SEQ, HEAD_DIM), F32),
        compiler_params=_params(("arbitrary", "arbitrary", "arbitrary"), blocks, extra=_nbytes((QKV_TM, tn), F32)),
        name="qkv_proj",
    )(h, w)


DT_TS = 1024


def _dt_kernel(h_ref, wt_ref, bias_ref, o_ref):
    raw = lax.dot_general(wt_ref[...], h_ref[0], (((1,), (1,)), ((), ())), preferred_element_type=F32)
    v = raw + bias_ref[...]
    o_ref[0] = jnp.maximum(v, 0.0) + jnp.log1p(jnp.exp(-jnp.abs(v)))


def _dt_proj(h, w_dt_t, dt_bias):
    blocks = [((DT_TS, D_MODEL), BF16), ((SSM_HEADS, D_MODEL), BF16), ((SSM_HEADS, DT_TS), F32)]
    return pl.pallas_call(
        _dt_kernel,
        grid=(BATCH, SEQ // DT_TS),
        in_specs=[pl.BlockSpec((1, DT_TS, D_MODEL), lambda b, i: (b, i, 0)),
                  pl.BlockSpec((SSM_HEADS, D_MODEL), lambda b, i: (0, 0)),
                  pl.BlockSpec((SSM_HEADS, 1), lambda b, i: (0, 0))],
        out_specs=pl.BlockSpec((1, SSM_HEADS, DT_TS), lambda b, i: (b, 0, i)),
        out_shape=jax.ShapeDtypeStruct((BATCH, SSM_HEADS, SEQ), F32),
        compiler_params=_params(("arbitrary", "arbitrary"), blocks),
        name="dt_proj",
    )(h, w_dt_t, dt_bias.reshape(SSM_HEADS, 1))


ATTN_SPAN = 2048
SUB_BLOCKS = ATTN_SPAN // ATTN_BLK


def _bucket_map(dil):
    qi = np.arange(ATTN_BLK)[:, None]
    kj = np.arange(2 * ATTN_BLK)[None, :]
    dist = np.clip(ATTN_BLK + qi - kj, 0, ATTN_BLK) * dil
    exact = REL_BUCKETS // 2
    n = np.maximum(dist, 1).astype(np.float32)
    large = exact + (np.log(n / np.float32(exact)) / np.float32(math.log(REL_MAX_DIST / exact))
                     * np.float32(REL_BUCKETS - exact)).astype(np.int32)
    large = np.minimum(large, REL_BUCKETS - 1)
    return np.where(dist < exact, dist, large).astype(np.int32)


def _attn_kernel(tab_ref, bucket_ref, q_ref, kc_ref, kp_ref, vc_ref, vp_ref, o_ref, l_ref, bias_ref, *, dil, group):
    b, n, hd = pl.program_id(0), pl.program_id(1), pl.program_id(2)

    @pl.when((b == 0) & (n == 0))
    def _():
        bucket = bucket_ref[...]
        bias = jnp.zeros((ATTN_BLK, 2 * ATTN_BLK), F32)
        for k in range(REL_BUCKETS):
            bias = jnp.where(bucket == k, tab_ref[k, group * HEADS_PER_GROUP + hd], bias)
        bias_ref[hd] = bias

    qi = lax.broadcasted_iota(jnp.int32, (ATTN_BLK, ATTN_BLK), 0)
    kj = lax.broadcasted_iota(jnp.int32, (ATTN_BLK, ATTN_BLK), 1)
    cur_ok = kj <= qi
    prev_in = kj >= qi
    prev_first = prev_in & (n > 0)
    bias_p = bias_ref[hd, :, :ATTN_BLK]
    bias_c = bias_ref[hd, :, ATTN_BLK:]
    scale = 1.0 / math.sqrt(HEAD_DIM)
    nt = (((1,), (1,)), ((), ()))
    per_residue = SUB_BLOCKS // dil

    def rows(r, m):
        start = r + m * ATTN_BLK * dil
        return pl.ds(start, ATTN_BLK, stride=dil) if dil > 1 else pl.ds(start, ATTN_BLK)

    for r in range(dil):
        k_prev = kp_ref[0, 0, rows(r, per_residue - 1), :].astype(BF16)
        v_prev = vp_ref[0, 0, rows(r, per_residue - 1), :].astype(BF16)
        for m in range(per_residue):
            sl = rows(r, m)
            q = q_ref[0, 0, sl, :].astype(BF16)
            k_cur = kc_ref[0, 0, sl, :].astype(BF16)
            v_cur = vc_ref[0, 0, sl, :].astype(BF16)
            s_p = lax.dot_general(q, k_prev, nt, preferred_element_type=F32) * scale
            s_c = lax.dot_general(q, k_cur, nt, preferred_element_type=F32) * scale
            s_p = jnp.where(prev_first if m == 0 else prev_in, s_p + bias_p, -jnp.inf)
            s_c = jnp.where(cur_ok, s_c + bias_c, -jnp.inf)
            mx = jnp.maximum(jnp.max(s_p, axis=-1, keepdims=True), jnp.max(s_c, axis=-1, keepdims=True))
            p_p = jnp.exp(s_p - mx)
            p_c = jnp.exp(s_c - mx)
            den = jnp.sum(p_p, axis=-1, keepdims=True) + jnp.sum(p_c, axis=-1, keepdims=True)
            acc = jnp.dot(p_p.astype(BF16), v_prev, preferred_element_type=F32)
            acc = acc + jnp.dot(p_c.astype(BF16), v_cur, preferred_element_type=F32)
            o_ref[0, 0, sl, :] = acc / den
            l_ref[0, 0, sl, :] = jnp.broadcast_to(mx + jnp.log(den), (ATTN_BLK, HEAD_DIM))
            k_prev, v_prev = k_cur, v_cur


def _attention_group(qkv, rel_bias, g):
    dil = DILATED_GROUPS[g][1]
    n_heads = N_ATTN_GROUPS * HEADS_PER_GROUP
    blk = (1, 1, ATTN_SPAN, HEAD_DIM)

    def spec(which, prev):
        base = which * n_heads + g * HEADS_PER_GROUP
        if prev:
            return pl.BlockSpec(blk, lambda b, n, hd: (b, base + hd, jnp.maximum(n - 1, 0), 0))
        return pl.BlockSpec(blk, lambda b, n, hd: (b, base + hd, n, 0))

    out_spec = pl.BlockSpec(blk, lambda b, n, hd: (b, hd, n, 0))
    out_sds = jax.ShapeDtypeStruct((BATCH, HEADS_PER_GROUP, SEQ, HEAD_DIM), F32)
    blocks = [(blk, F32)] * 7 + [((ATTN_BLK, 2 * ATTN_BLK), jnp.int32)]
    return pl.pallas_call(
        functools.partial(_attn_kernel, dil=dil, group=g),
        grid=(BATCH, SEQ // ATTN_SPAN, HEADS_PER_GROUP),
        in_specs=[pl.BlockSpec(memory_space=pltpu.SMEM),
                  pl.BlockSpec((ATTN_BLK, 2 * ATTN_BLK), lambda b, n, hd: (0, 0)),
                  spec(0, False), spec(1, False), spec(1, True), spec(2, False), spec(2, True)],
        out_specs=[out_spec, out_spec],
        out_shape=[out_sds, out_sds],
        scratch_shapes=[pltpu.VMEM((HEADS_PER_GROUP, ATTN_BLK, 2 * ATTN_BLK), F32)],
        compiler_params=_params(("arbitrary", "arbitrary", "arbitrary"), blocks,
                                extra=_nbytes((HEADS_PER_GROUP, ATTN_BLK, 2 * ATTN_BLK), F32)),
        name=f"dilated_attn_g{g}",
    )(rel_bias, jnp.asarray(_bucket_map(dil)), qkv, qkv, qkv, qkv, qkv)


N_CHUNKS = SEQ // CHUNK
HIST = 8


def _split3(v):
    hi = v.astype(BF16)
    r1 = v - hi.astype(F32)
    mid = r1.astype(BF16)
    lo = (r1 - mid.astype(F32)).astype(BF16)
    return hi, mid, lo


def _ssd_kernel(xs_ref, bm_ref, cm_ref, z_ref, dtt_ref, cw_ref, cb_ref, acol_ref, dskip_ref, nw_ref,
                y_ref, state_ref, ext_ref):
    state_ref[...] = jnp.zeros_like(state_ref)
    ext_ref[0:HIST, :] = jnp.zeros((HIST, CONV_GROUP_W), F32)

    li = lax.broadcasted_iota(jnp.int32, (CHUNK, CHUNK), 0)
    si = lax.broadcasted_iota(jnp.int32, (CHUNK, CHUNK), 1)
    tril = li >= si
    ltri = jnp.where(tril, 1.0, 0.0).astype(BF16)
    utri = jnp.where(li <= si, 1.0, 0.0).astype(BF16)
    eye = jnp.where(li == si, 1.0, 0.0).astype(BF16)
    low_half = si < SSM_HEAD_DIM
    nt = (((1,), (1,)), ((), ()))

    cw = cw_ref[0]
    cb = cb_ref[0]
    a_col = -jnp.exp(acol_ref[0])
    dskip = dskip_ref[0]
    nw = nw_ref[...]

    def to_cols(vt):
        return sum(lax.dot_general(eye, p, nt, preferred_element_type=F32) for p in _split3(vt))

    def expand(v):
        blocks = []
        for k in range(HEADS_PER_SSM_GROUP // 2):
            a = jnp.broadcast_to(v[:, 2 * k:2 * k + 1], (CHUNK, LANES))
            b = jnp.broadcast_to(v[:, 2 * k + 1:2 * k + 2], (CHUNK, LANES))
            blocks.append(jnp.where(low_half, a, b))
        return jnp.concatenate(blocks, axis=1)

    def chunk(c, carry):
        r0 = pl.multiple_of(c * CHUNK, CHUNK)
        rows = pl.ds(r0, CHUNK)
        ext_ref[HIST:, 0:GROUP_W] = xs_ref[0, rows, :].astype(F32)
        ext_ref[HIST:, GROUP_W:GROUP_W + D_STATE] = bm_ref[0, rows, :].astype(F32)
        ext_ref[HIST:, GROUP_W + D_STATE:] = cm_ref[0, rows, :].astype(F32)
        acc = cb + cw[CONV_K - 1:CONV_K, :] * ext_ref[HIST:, :]
        for s in range(1, CONV_K):
            acc = acc + cw[CONV_K - 1 - s:CONV_K - s, :] * ext_ref[HIST - s:HIST - s + CHUNK, :]
        ext_ref[0:HIST, :] = ext_ref[CHUNK:CHUNK + HIST, :]
        xbc = _silu(acc)
        x = xbc[:, :GROUP_W]
        bmat = xbc[:, GROUP_W:GROUP_W + D_STATE]
        cmat = xbc[:, GROUP_W + D_STATE:].astype(BF16)

        dtt = dtt_ref[0, :, rows]
        dat = dtt * a_col
        dat3 = _split3(dat)
        acum_t = sum(jnp.dot(p, utri, preferred_element_type=F32) for p in dat3)
        acum = sum(lax.dot_general(ltri, p, nt, preferred_element_type=F32) for p in dat3)
        dt = to_cols(dtt)
        expa = jnp.exp(acum)
        decay = jnp.exp(acum[CHUNK - 1:CHUNK, :] - acum)
        dt_x = expand(dt)
        expa_x = expand(expa)
        decay_x = expand(decay)

        bt = bmat.T.astype(BF16)
        cbm = jnp.dot(cmat, bt, preferred_element_type=F32)
        xdt = x * dt_x
        xdt_b = xdt.astype(BF16)

        y_blocks = []
        for k in range(HEADS_PER_SSM_GROUP // 2):
            xk = xdt_b[:, k * LANES:(k + 1) * LANES]
            ms = []
            for r in (2 * k, 2 * k + 1):
                col = jnp.broadcast_to(acum[:, r:r + 1], (CHUNK, CHUNK))
                row = jnp.broadcast_to(acum_t[r:r + 1, :], (CHUNK, CHUNK))
                ldec = jnp.exp(jnp.where(tril, col - row, -jnp.inf))
                ms.append((cbm * ldec).astype(BF16))
            lhs = jnp.concatenate(ms, axis=1)
            rhs = jnp.concatenate([jnp.where(low_half, xk, jnp.zeros_like(xk)),
                                   jnp.where(low_half, jnp.zeros_like(xk), xk)], axis=0)
            y_blocks.append(jnp.dot(lhs, rhs, preferred_element_type=F32))
        y = jnp.concatenate(y_blocks, axis=1)

        state = state_ref[...]
        y = y + jnp.dot(cmat, state.astype(BF16), preferred_element_type=F32) * expa_x
        xw = (xdt * decay_x).astype(BF16)
        state_ref[...] = state * expa_x[CHUNK - 1:CHUNK, :] + jnp.dot(bt, xw, preferred_element_type=F32)

        y = y + dskip * x
        y = y * _silu(z_ref[0, rows, :].astype(F32))
        y = y * lax.rsqrt(jnp.mean(y * y, axis=-1, keepdims=True) + EPS)
        y_ref[0, rows, :] = (y * nw).astype(y_ref.dtype)
        return carry

    lax.fori_loop(0, N_CHUNKS, chunk, 0)


def _ssd(proj, dtt, conv_w_g, conv_b_g, a_col, dskip_x, norm_w):
    view = proj.reshape(BATCH, SEQ, PROJ_W)
    gw, ns = GROUP_W, D_STATE
    blocks = [((SEQ, gw), BF16)] * 3 + [((SEQ, ns), BF16)] * 2 + [((HEADS_PER_SSM_GROUP, SEQ), F32)]
    scratch_bytes = _nbytes((ns, gw), F32) + _nbytes((CHUNK + HIST, CONV_GROUP_W), F32)
    return pl.pallas_call(
        _ssd_kernel,
        grid=(BATCH, SSM_GROUPS),
        in_specs=[pl.BlockSpec((1, SEQ, gw), lambda b, g: (b, 0, P_XBC // gw + g)),
                  pl.BlockSpec((1, SEQ, ns), lambda b, g: (b, 0, P_B // ns + g)),
                  pl.BlockSpec((1, SEQ, ns), lambda b, g: (b, 0, P_C // ns + g)),
                  pl.BlockSpec((1, SEQ, gw), lambda b, g: (b, 0, P_Z // gw + g)),
                  pl.BlockSpec((1, HEADS_PER_SSM_GROUP, SEQ), lambda b, g: (b, g, 0)),
                  pl.BlockSpec((1, CONV_K, CONV_GROUP_W), lambda b, g: (g, 0, 0)),
                  pl.BlockSpec((1, 1, CONV_GROUP_W), lambda b, g: (g, 0, 0)),
                  pl.BlockSpec((1, HEADS_PER_SSM_GROUP, LANES), lambda b, g: (g, 0, 0)),
                  pl.BlockSpec((1, 1, gw), lambda b, g: (g, 0, 0)),
                  pl.BlockSpec((1, gw), lambda b, g: (0, g))],
        out_specs=pl.BlockSpec((1, SEQ, gw), lambda b, g: (b, 0, g)),
        out_shape=jax.ShapeDtypeStruct((BATCH, SEQ, D_INNER), BF16),
        scratch_shapes=[pltpu.VMEM((ns, gw), F32), pltpu.VMEM((CHUNK + HIST, CONV_GROUP_W), F32)],
        compiler_params=_params(("arbitrary", "arbitrary"), blocks, extra=scratch_bytes + (4 << 20)),
        name="ssd_scan",
    )(view, view, view, view, dtt, conv_w_g, conv_b_g, a_col, dskip_x, norm_w.reshape(1, D_INNER))


MG_TM = 512
MG_TN = 512


def _merge_kernel(o0_ref, o1_ref, o2_ref, l0_ref, l1_ref, l2_ref, y_ref, ga_ref, gs_ref, wa_ref, ws_ref,
                  out_ref, attn_ref):
    @pl.when(pl.program_id(1) == 0)
    def _():
        for hd in range(HEADS_PER_GROUP):
            l0, l1, l2 = l0_ref[0, hd], l1_ref[0, hd], l2_ref[0, hd]
            m = jnp.maximum(jnp.maximum(l0, l1), l2)
            e0, e1, e2 = jnp.exp(l0 - m), jnp.exp(l1 - m), jnp.exp(l2 - m)
            den = e0 + e1 + e2
            attn = (e0 / den) * o0_ref[0, hd] + (e1 / den) * o1_ref[0, hd] + (e2 / den) * o2_ref[0, hd]
            attn_ref[:, hd * HEAD_DIM:(hd + 1) * HEAD_DIM] = attn.astype(BF16)

    pa = jnp.dot(attn_ref[...], wa_ref[...], preferred_element_type=F32)
    ps = jnp.dot(y_ref[...], ws_ref[...], preferred_element_type=F32)
    merged = _sigmoid(ga_ref[...].astype(F32)) * pa + _sigmoid(gs_ref[...].astype(F32)) * ps
    out_ref[...] = merged.astype(out_ref.dtype)


def _merge_proj(outs, lses, y, proj, w_attn, w_ssm):
    row = lambda i, j: (i, 0)
    tiles_per_seq = SEQ // MG_TM
    a_spec = pl.BlockSpec((1, HEADS_PER_GROUP, MG_TM, HEAD_DIM),
                          lambda i, j: (i // tiles_per_seq, 0, i % tiles_per_seq, 0))
    blocks = [((MG_TM, ATTN_OUT), F32)] * 6 + [((MG_TM, D_INNER), BF16)] + [((MG_TM, MG_TN), BF16)] * 3 \
        + [((ATTN_OUT, MG_TN), BF16), ((D_INNER, MG_TN), BF16)]
    return pl.pallas_call(
        _merge_kernel,
        grid=(TOKENS // MG_TM, D_MODEL // MG_TN),
        in_specs=[a_spec] * 6 + [
            pl.BlockSpec((MG_TM, D_INNER), row),
            pl.BlockSpec((MG_TM, MG_TN), lambda i, j: (i, P_GA // MG_TN + j)),
            pl.BlockSpec((MG_TM, MG_TN), lambda i, j: (i, P_GS // MG_TN + j)),
            pl.BlockSpec((ATTN_OUT, MG_TN), lambda i, j: (0, j)),
            pl.BlockSpec((D_INNER, MG_TN), lambda i, j: (0, j))],
        out_specs=pl.BlockSpec((MG_TM, MG_TN), lambda i, j: (i, j)),
        out_shape=jax.ShapeDtypeStruct((TOKENS, D_MODEL), BF16),
        scratch_shapes=[pltpu.VMEM((MG_TM, ATTN_OUT), BF16)],
        compiler_params=_params(("arbitrary", "arbitrary"), blocks, extra=4 * _nbytes((MG_TM, MG_TN), F32)),
        name="merge_proj",
    )(*outs, *lses, y, proj, proj, w_attn, w_ssm)


RS_TM = 1024


def _resid_kernel(a_ref, w_ref, x_ref, g_ref, o_ref):
    o_ref[...] = x_ref[...] + g_ref[0] * jnp.dot(a_ref[...], w_ref[...], preferred_element_type=F32)


def _resid_proj(a, w, x, gate, name):
    k = a.shape[1]
    tn = 1024 if k <= D_MODEL else 512
    blocks = [((RS_TM, k), BF16), ((k, tn), BF16), ((RS_TM, tn), F32), ((RS_TM, tn), F32)]
    return pl.pallas_call(
        _resid_kernel,
        grid=(TOKENS // RS_TM, D_MODEL // tn),
        in_specs=[pl.BlockSpec((RS_TM, k), lambda i, j: (i, 0)),
                  pl.BlockSpec((k, tn), lambda i, j: (0, j)),
                  pl.BlockSpec((RS_TM, tn), lambda i, j: (i, j)),
                  pl.BlockSpec((1, 1, tn), lambda i, j: (i * RS_TM // SEQ, 0, j))],
        out_specs=pl.BlockSpec((RS_TM, tn), lambda i, j: (i, j)),
        out_shape=jax.ShapeDtypeStruct((TOKENS, D_MODEL), F32),
        compiler_params=_params(("arbitrary", "arbitrary"), blocks, extra=_nbytes((RS_TM, tn), F32)),
        name=name,
    )(a, w, x, gate)


FF_TM = 1024
FF_TN = 512


def _swiglu_kernel(h_ref, wg_ref, wu_ref, o_ref):
    h = h_ref[...]
    hg = jnp.dot(h, wg_ref[...], preferred_element_type=F32)
    hu = jnp.dot(h, wu_ref[...], preferred_element_type=F32)
    o_ref[...] = (_silu(hg) * hu).astype(o_ref.dtype)


def _swiglu_in(h, w):
    nblk = D_FF // FF_TN
    blocks = [((FF_TM, D_MODEL), BF16), ((D_MODEL, FF_TN), BF16), ((D_MODEL, FF_TN), BF16), ((FF_TM, FF_TN), BF16)]
    return pl.pallas_call(
        _swiglu_kernel,
        grid=(TOKENS // FF_TM, nblk),
        in_specs=[pl.BlockSpec((FF_TM, D_MODEL), lambda i, j: (i, 0)),
                  pl.BlockSpec((D_MODEL, FF_TN), lambda i, j: (0, j)),
                  pl.BlockSpec((D_MODEL, FF_TN), lambda i, j: (0, nblk + j))],
        out_specs=pl.BlockSpec((FF_TM, FF_TN), lambda i, j: (i, j)),
        out_shape=jax.ShapeDtypeStruct((TOKENS, D_FF), BF16),
        compiler_params=_params(("arbitrary", "arbitrary"), blocks, extra=3 * _nbytes((FF_TM, FF_TN), F32)),
        name="swiglu_in",
    )(h, w, w)


def _mixer(x, h, l, rel_bias, w_in, conv_w, conv_b, dt_bias, a_log, d_skip, ssm_norm_w,
           w_attn_proj, w_ssm_proj, w_out, gate):
    w = w_in[l]
    w_qkv = w[:, :OFF_Z].astype(BF16)
    w_rest = jnp.concatenate([w[:, OFF_Z:OFF_DT], w[:, OFF_GA:]], axis=1).astype(BF16)
    w_dt_t = w[:, OFF_DT:OFF_GA].T.astype(BF16)
    qkv = _qkv_proj(h, w_qkv)
    proj = _matmul(h.reshape(TOKENS, D_MODEL), w_rest, BF16, "in_proj")
    dtt = _dt_proj(h, w_dt_t, dt_bias[l])

    outs, lses = zip(*[_attention_group(qkv, rel_bias, g) for g in range(N_ATTN_GROUPS)])

    cw, cbias = conv_w[l], conv_b[l]
    xs_w = cw[:, :D_INNER].reshape(CONV_K, SSM_GROUPS, GROUP_W)
    b_w = cw[:, D_INNER:D_INNER + SSM_GROUPS * D_STATE].reshape(CONV_K, SSM_GROUPS, D_STATE)
    c_w = cw[:, D_INNER + SSM_GROUPS * D_STATE:].reshape(CONV_K, SSM_GROUPS, D_STATE)
    conv_w_g = jnp.concatenate([xs_w, b_w, c_w], axis=2).transpose(1, 0, 2)
    conv_b_g = jnp.concatenate([cbias[:D_INNER].reshape(SSM_GROUPS, GROUP_W),
                                cbias[D_INNER:D_INNER + SSM_GROUPS * D_STATE].reshape(SSM_GROUPS, D_STATE),
                                cbias[D_INNER + SSM_GROUPS * D_STATE:].reshape(SSM_GROUPS, D_STATE)],
                               axis=1).reshape(SSM_GROUPS, 1, CONV_GROUP_W)
    a_col = jnp.broadcast_to(a_log[l].reshape(SSM_GROUPS, HEADS_PER_SSM_GROUP, 1),
                             (SSM_GROUPS, HEADS_PER_SSM_GROUP, LANES))
    dskip_x = jnp.repeat(d_skip[l], SSM_HEAD_DIM).reshape(SSM_GROUPS, 1, GROUP_W)
    y = _ssd(proj, dtt, conv_w_g, conv_b_g, a_col, dskip_x, ssm_norm_w[l])

    merged = _merge_proj(outs, lses, y.reshape(TOKENS, D_INNER), proj,
                         w_attn_proj[l].astype(BF16), w_ssm_proj[l].astype(BF16))
    return _resid_proj(merged, w_out[l].astype(BF16), x, gate, "out_proj_resid")


def kernel(x, c, rel_bias, norm1_w, norm2_w, w_mod, b_mod, w_in, conv_w, conv_b, dt_bias, a_log, d_skip,
           ssm_norm_w, w_attn_proj, w_ssm_proj, w_out, w_ffn_in, w_ffn_out, final_norm_w):
    mod = _modulation(c, w_mod, b_mod)[:, :BATCH]
    xt = x.reshape(TOKENS, D_MODEL)
    for l in range(DEPTH):
        sh1, sc1, g1, sh2, sc2, g2 = [m.reshape(BATCH, 1, D_MODEL) for m in jnp.split(mod[l], 6, axis=-1)]
        h = _norm(xt.reshape(BATCH, SEQ, D_MODEL), norm1_w[l], sc1, sh1, BF16, True)
        xt = _mixer(xt, h, l, rel_bias, w_in, conv_w, conv_b, dt_bias, a_log, d_skip, ssm_norm_w,
                    w_attn_proj, w_ssm_proj, w_out, g1)
        h = _norm(xt.reshape(BATCH, SEQ, D_MODEL), norm2_w[l], sc2, sh2, BF16, True)
        u = _swiglu_in(h.reshape(TOKENS, D_MODEL), w_ffn_in[l].astype(BF16))
        xt = _resid_proj(u, w_ffn_out[l].astype(BF16), xt, g2, "ffn_out_resid")
    zeros = jnp.zeros((BATCH, 1, D_MODEL), F32)
    return _norm(xt.reshape(BATCH, SEQ, D_MODEL), final_norm_w, zeros, zeros, F32, False)
```

```python
import functools
import math

import jax
import jax.numpy as jnp
import numpy as np
from jax import lax
from jax.experimental import pallas as pl
from jax.experimental.pallas import tpu as pltpu

F32 = jnp.float32
BF16 = jnp.bfloat16

D_MODEL = 2048
BATCH = 4
SEQ = 4096
TOKENS = BATCH * SEQ
DEPTH = 2
DILATED_GROUPS = ((128, 1), (512, 4), (2048, 16))
N_ATTN_GROUPS = len(DILATED_GROUPS)
HEADS_PER_GROUP = 4
N_ATTN_HEADS = N_ATTN_GROUPS * HEADS_PER_GROUP
HEAD_DIM = 128
QKV_W = N_ATTN_HEADS * HEAD_DIM
ATTN_OUT = HEADS_PER_GROUP * HEAD_DIM
ATTN_BLK = 128
REL_BUCKETS = 32
REL_MAX_DIST = 2048
D_INNER = 2 * D_MODEL
SSM_HEAD_DIM = 64
SSM_HEADS = D_INNER // SSM_HEAD_DIM
SSM_GROUPS = 8
HEADS_PER_SSM_GROUP = SSM_HEADS // SSM_GROUPS
GROUP_W = HEADS_PER_SSM_GROUP * SSM_HEAD_DIM
D_STATE = 128
CONV_K = 4
CHUNK = 128
CONV_DIM = D_INNER + 2 * SSM_GROUPS * D_STATE
CONV_GROUP_W = GROUP_W + 2 * D_STATE
D_FF = 5632
SPLITS = (QKV_W, QKV_W, QKV_W, D_INNER, CONV_DIM, SSM_HEADS, D_MODEL, D_MODEL)
OFF_Q, OFF_K, OFF_V, OFF_Z, OFF_XBC, OFF_DT, OFF_GA, OFF_GS = np.cumsum((0,) + SPLITS[:-1]).tolist()
N_QKV_HEADS = 3 * N_ATTN_HEADS
PROJ_W = D_INNER + CONV_DIM + 2 * D_MODEL
P_Z = 0
P_XBC = D_INNER
P_B = P_XBC + D_INNER
P_C = P_B + SSM_GROUPS * D_STATE
P_GA = P_XBC + CONV_DIM
P_GS = P_GA + D_MODEL
EPS = 1e-6
LOG2_E = math.log2(math.e)

LANES = 128
VMEM_LIMIT_CAP = 60 * 1024 * 1024


def _nbytes(shape, dtype):
    return int(np.prod(shape)) * jnp.dtype(dtype).itemsize


def _params(sem, blocks, extra=0):
    need = 2 * sum(_nbytes(s, d) for s, d in blocks) + extra
    return pltpu.CompilerParams(dimension_semantics=sem,
                                vmem_limit_bytes=min(VMEM_LIMIT_CAP, need + (8 << 20)))


def _silu(v):
    return v / (1.0 + jnp.exp(-v))


def _sigmoid(v):
    return 1.0 / (1.0 + jnp.exp(-v))


MOD_TN = 1024
MOD_ROWS = 8


def _mod_kernel(c_ref, w_ref, b_ref, o_ref):
    ca = _silu(c_ref[...])
    o_ref[0] = jnp.dot(ca, w_ref[0], preferred_element_type=F32) + b_ref[0]


def _modulation(c, w_mod, b_mod):
    c8 = jnp.zeros((MOD_ROWS, D_MODEL), F32).at[:BATCH].set(c)
    n = 6 * D_MODEL
    blocks = [((MOD_ROWS, D_MODEL), F32), ((D_MODEL, MOD_TN), F32), ((1, MOD_TN), F32), ((MOD_ROWS, MOD_TN), F32)]
    return pl.pallas_call(
        _mod_kernel,
        grid=(DEPTH, n // MOD_TN),
        in_specs=[pl.BlockSpec((MOD_ROWS, D_MODEL), lambda l, j: (0, 0)),
                  pl.BlockSpec((1, D_MODEL, MOD_TN), lambda l, j: (l, 0, j)),
                  pl.BlockSpec((1, 1, MOD_TN), lambda l, j: (l, 0, j))],
        out_specs=pl.BlockSpec((1, MOD_ROWS, MOD_TN), lambda l, j: (l, 0, j)),
        out_shape=jax.ShapeDtypeStruct((DEPTH, MOD_ROWS, n), F32),
        compiler_params=_params(("arbitrary", "arbitrary"), blocks),
        name="modulation",
    )(c8, w_mod, b_mod.reshape(DEPTH, 1, n))


NORM_TS = 512


def _norm_kernel(x_ref, w_ref, sc_ref, sh_ref, o_ref, *, modulate):
    x = x_ref[0]
    y = x * lax.rsqrt(jnp.mean(x * x, axis=-1, keepdims=True) + EPS)
    y = y * w_ref[...]
    if modulate:
        y = y * (1.0 + sc_ref[0]) + sh_ref[0]
    o_ref[0] = y.astype(o_ref.dtype)


def _norm(x, w, sc, sh, out_dtype, modulate):
    blocks = [((NORM_TS, D_MODEL), F32), ((NORM_TS, D_MODEL), out_dtype)]
    return pl.pallas_call(
        functools.partial(_norm_kernel, modulate=modulate),
        grid=(BATCH, SEQ // NORM_TS),
        in_specs=[pl.BlockSpec((1, NORM_TS, D_MODEL), lambda b, i: (b, i, 0)),
                  pl.BlockSpec((1, D_MODEL), lambda b, i: (0, 0)),
                  pl.BlockSpec((1, 1, D_MODEL), lambda b, i: (b, 0, 0)),
                  pl.BlockSpec((1, 1, D_MODEL), lambda b, i: (b, 0, 0))],
        out_specs=pl.BlockSpec((1, NORM_TS, D_MODEL), lambda b, i: (b, i, 0)),
        out_shape=jax.ShapeDtypeStruct((BATCH, SEQ, D_MODEL), out_dtype),
        compiler_params=_params(("arbitrary", "arbitrary"), blocks, extra=3 * _nbytes((NORM_TS, D_MODEL), F32)),
        name="rmsnorm_mod" if modulate else "rmsnorm",
    )(x, w.reshape(1, D_MODEL), sc, sh)


MM_TM = 1024
MM_TN = 1024


def _mm_kernel(a_ref, w_ref, o_ref):
    o_ref[...] = jnp.dot(a_ref[...], w_ref[...], preferred_element_type=F32).astype(o_ref.dtype)


def _matmul(a, w, out_dtype, name):
    m, k = a.shape
    n = w.shape[1]
    blocks = [((MM_TM, k), BF16), ((k, MM_TN), BF16), ((MM_TM, MM_TN), out_dtype)]
    return pl.pallas_call(
        _mm_kernel,
        grid=(m // MM_TM, n // MM_TN),
        in_specs=[pl.BlockSpec((MM_TM, k), lambda i, j: (i, 0)),
                  pl.BlockSpec((k, MM_TN), lambda i, j: (0, j))],
        out_specs=pl.BlockSpec((MM_TM, MM_TN), lambda i, j: (i, j)),
        out_shape=jax.ShapeDtypeStruct((m, n), out_dtype),
        compiler_params=_params(("arbitrary", "arbitrary"), blocks, extra=_nbytes((MM_TM, MM_TN), F32)),
        name=name,
    )(a, w)


QKV_TM = 2048
QKV_HEADS_PER_STEP = 4


def _qkv_kernel(h_ref, w_ref, o_ref):
    res = jnp.dot(h_ref[0], w_ref[...], preferred_element_type=F32)
    for hd in range(QKV_HEADS_PER_STEP):
        o_ref[0, hd] = res[:, hd * HEAD_DIM:(hd + 1) * HEAD_DIM]


def _qkv_proj(h, w):
    tn = QKV_HEADS_PER_STEP * HEAD_DIM
    blocks = [((QKV_TM, D_MODEL), BF16), ((D_MODEL, tn), BF16), ((QKV_TM, tn), F32)]
    return pl.pallas_call(
        _qkv_kernel,
        grid=(BATCH, SEQ // QKV_TM, N_QKV_HEADS // QKV_HEADS_PER_STEP),
        in_specs=[pl.BlockSpec((1, QKV_TM, D_MODEL), lambda b, i, j: (b, i, 0)),
                  pl.BlockSpec((D_MODEL, tn), lambda b, i, j: (0, j))],
        out_specs=pl.BlockSpec((1, QKV_HEADS_PER_STEP, QKV_TM, HEAD_DIM), lambda b, i, j: (b, j, i, 0)),
        out_shape=jax.ShapeDtypeStruct((BATCH, N_QKV_HEADS, SEQ, HEAD_DIM), F32),
        compiler_params=_params(("arbitrary", "arbitrary", "arbitrary"), blocks, extra=_nbytes((QKV_TM, tn), F32)),
        name="qkv_proj",
    )(h, w)


DT_TS = 1024


def _dt_kernel(h_ref, wt_ref, bias_ref, o_ref):
    raw = lax.dot_general(wt_ref[...], h_ref[0], (((1,), (1,)), ((), ())), preferred_element_type=F32)
    v = raw + bias_ref[...]
    o_ref[0] = jnp.maximum(v, 0.0) + jnp.log1p(jnp.exp(-jnp.abs(v)))


def _dt_proj(h, w_dt_t, dt_bias):
    blocks = [((DT_TS, D_MODEL), BF16), ((SSM_HEADS, D_MODEL), BF16), ((SSM_HEADS, DT_TS), F32)]
    return pl.pallas_call(
        _dt_kernel,
        grid=(BATCH, SEQ // DT_TS),
        in_specs=[pl.BlockSpec((1, DT_TS, D_MODEL), lambda b, i: (b, i, 0)),
                  pl.BlockSpec((SSM_HEADS, D_MODEL), lambda b, i: (0, 0)),
                  pl.BlockSpec((SSM_HEADS, 1), lambda b, i: (0, 0))],
        out_specs=pl.BlockSpec((1, SSM_HEADS, DT_TS), lambda b, i: (b, 0, i)),
        out_shape=jax.ShapeDtypeStruct((BATCH, SSM_HEADS, SEQ), F32),
        compiler_params=_params(("arbitrary", "arbitrary"), blocks),
        name="dt_proj",
    )(h, w_dt_t, dt_bias.reshape(SSM_HEADS, 1))


ATTN_SPAN = 2048
SUB_BLOCKS = ATTN_SPAN // ATTN_BLK
MERGE_ROWS = 256


def _bucket_maps():
    qi = np.arange(ATTN_BLK)[:, None]
    kj = np.arange(2 * ATTN_BLK)[None, :]
    steps = np.clip(ATTN_BLK + qi - kj, 0, ATTN_BLK)
    exact = REL_BUCKETS // 2
    maps = []
    for _, dil in DILATED_GROUPS:
        dist = steps * dil
        n = np.maximum(dist, 1).astype(np.float32)
        large = exact + (np.log(n / np.float32(exact)) / np.float32(math.log(REL_MAX_DIST / exact))
                         * np.float32(REL_BUCKETS - exact)).astype(np.int32)
        large = np.minimum(large, REL_BUCKETS - 1)
        maps.append(np.where(dist < exact, dist, large).astype(np.int32))
    return np.stack(maps)


def _attn_kernel(tab_ref, bucket_ref, *refs):
    qkv_refs = refs[:5 * N_ATTN_GROUPS]
    out_ref, o_scr, l_scr, bias_ref, s_scr, p_scr, d_scr = refs[5 * N_ATTN_GROUPS:]
    b, hd, n = pl.program_id(0), pl.program_id(1), pl.program_id(2)

    @pl.when((b == 0) & (n == 0))
    def _():
        for g in range(N_ATTN_GROUPS):
            bucket = bucket_ref[g]
            bias = jnp.zeros((ATTN_BLK, 2 * ATTN_BLK), F32)
            for k in range(REL_BUCKETS):
                bias = jnp.where(bucket == k, tab_ref[k, g * HEADS_PER_GROUP + hd], bias)
            bias_ref[g * HEADS_PER_GROUP + hd] = bias

    kj2 = lax.broadcasted_iota(jnp.int32, (ATTN_BLK, 2 * ATTN_BLK), 1)
    qi2 = lax.broadcasted_iota(jnp.int32, (ATTN_BLK, 2 * ATTN_BLK), 0)
    band = (kj2 >= qi2) & (kj2 <= qi2 + ATTN_BLK)
    band_first = band & ((kj2 >= ATTN_BLK) | (n > 0))
    scale = 1.0 / math.sqrt(HEAD_DIM)
    nt = (((1,), (1,)), ((), ()))

    for g, (_, dil) in enumerate(DILATED_GROUPS):
        q_ref, kc_ref, kp_ref, vc_ref, vp_ref = qkv_refs[5 * g:5 * g + 5]
        bias = bias_ref[g * HEADS_PER_GROUP + hd]
        per_residue = SUB_BLOCKS // dil

        def rows(r, m, dil=dil):
            start = r + m * ATTN_BLK * dil
            return pl.ds(start, ATTN_BLK, stride=dil) if dil > 1 else pl.ds(start, ATTN_BLK)

        def both(prev_ref, cur_ref, r, m):
            prev = prev_ref[0, 0, rows(r, per_residue - 1), :] if m == 0 else cur_ref[0, 0, rows(r, m - 1), :]
            return jnp.concatenate([prev.astype(BF16), cur_ref[0, 0, rows(r, m), :].astype(BF16)], axis=0)

        subs = [(r, m) for r in range(dil) for m in range(per_residue)]
        for i, (r, m) in enumerate(subs):
            q = q_ref[0, 0, rows(r, m), :].astype(BF16)
            s = lax.dot_general(q, both(kp_ref, kc_ref, r, m), nt, preferred_element_type=F32) * scale
            s_scr[i] = jnp.where(band_first if m == 0 else band, s + bias, -jnp.inf)
        for i, (r, m) in enumerate(subs):
            s = s_scr[i]
            mx = jnp.max(s, axis=-1, keepdims=True)
            p = jnp.exp(s - mx)
            den = jnp.sum(p, axis=-1, keepdims=True)
            p_scr[i] = p.astype(BF16)
            d_scr[i] = jnp.broadcast_to(den, (ATTN_BLK, HEAD_DIM))
            l_scr[g, rows(r, m), :] = jnp.broadcast_to(mx + jnp.log(den), (ATTN_BLK, HEAD_DIM))
        for i, (r, m) in enumerate(subs):
            acc = jnp.dot(p_scr[i], both(vp_ref, vc_ref, r, m), preferred_element_type=F32)
            o_scr[g, rows(r, m), :] = acc / d_scr[i]

    for c in range(ATTN_SPAN // MERGE_ROWS):
        sl = pl.ds(c * MERGE_ROWS, MERGE_ROWS)
        l0, l1, l2 = l_scr[0, sl, :], l_scr[1, sl, :], l_scr[2, sl, :]
        mx = jnp.maximum(jnp.maximum(l0, l1), l2)
        e0, e1, e2 = jnp.exp(l0 - mx), jnp.exp(l1 - mx), jnp.exp(l2 - mx)
        den = e0 + e1 + e2
        attn = (e0 / den) * o_scr[0, sl, :] + (e1 / den) * o_scr[1, sl, :] + (e2 / den) * o_scr[2, sl, :]
        out_ref[0, sl, :] = attn.astype(out_ref.dtype)


def _attention(qkv, rel_bias):
    blk = (1, 1, ATTN_SPAN, HEAD_DIM)

    def spec(which, g, prev):
        base = which * N_ATTN_HEADS + g * HEADS_PER_GROUP
        if prev:
            return pl.BlockSpec(blk, lambda b, hd, n: (b, base + hd, jnp.maximum(n - 1, 0), 0))
        return pl.BlockSpec(blk, lambda b, hd, n: (b, base + hd, n, 0))

    qkv_specs = []
    for g in range(N_ATTN_GROUPS):
        qkv_specs += [spec(0, g, False), spec(1, g, False), spec(1, g, True), spec(2, g, False), spec(2, g, True)]
    bias_shape = (N_ATTN_HEADS, ATTN_BLK, 2 * ATTN_BLK)
    scr_shape = (N_ATTN_GROUPS, ATTN_SPAN, HEAD_DIM)
    stage_shape = (SUB_BLOCKS, ATTN_BLK, 2 * ATTN_BLK)
    blocks =[(blk, F32)] * len(qkv_specs) + [((ATTN_SPAN, HEAD_DIM), BF16), (bias_shape, jnp.int32)]
    return pl.pallas_call(
        _attn_kernel,
        grid=(BATCH, HEADS_PER_GROUP, SEQ // ATTN_SPAN),
        in_specs=[pl.BlockSpec(memory_space=pltpu.SMEM),
                  pl.BlockSpec((N_ATTN_GROUPS, ATTN_BLK, 2 * ATTN_BLK), lambda b, hd, n: (0, 0, 0))] + qkv_specs,
        out_specs=pl.BlockSpec((1, ATTN_SPAN, HEAD_DIM), lambda b, hd, n: (b, n, hd)),
        out_shape=jax.ShapeDtypeStruct((BATCH, SEQ, ATTN_OUT), BF16),
        scratch_shapes=[pltpu.VMEM(scr_shape, F32), pltpu.VMEM(scr_shape, F32), pltpu.VMEM(bias_shape, F32),
                        pltpu.VMEM(stage_shape, F32), pltpu.VMEM(stage_shape, BF16),
                        pltpu.VMEM((SUB_BLOCKS, ATTN_BLK, HEAD_DIM), F32)],
        compiler_params=_params(("arbitrary", "arbitrary", "arbitrary"), blocks,
                                extra=2 * _nbytes(scr_shape, F32) + _nbytes(bias_shape, F32)
                                + _nbytes(stage_shape, F32) + _nbytes(stage_shape, BF16)
                                + _nbytes((SUB_BLOCKS, ATTN_BLK, HEAD_DIM), F32)),
        name="dilated_attn",
    )(rel_bias, jnp.asarray(_bucket_maps()), *([qkv] * len(qkv_specs)))


N_CHUNKS = SEQ // CHUNK
HEAD_PAIRS = HEADS_PER_SSM_GROUP // 2


def _split3(v):
    hi = v.astype(BF16)
    r1 = v - hi.astype(F32)
    mid = r1.astype(BF16)
    lo = (r1 - mid.astype(F32)).astype(BF16)
    return hi, mid, lo


def _ssd_kernel(xs_ref, bm_ref, cm_ref, z_ref, dtt_ref, cw_ref, cb_ref, acol_ref, dskip_ref, nw_ref,
                y_ref, state_ref):
    state_ref[...] = jnp.zeros_like(state_ref)

    li = lax.broadcasted_iota(jnp.int32, (CHUNK, CHUNK), 0)
    si = lax.broadcasted_iota(jnp.int32, (CHUNK, CHUNK), 1)
    tril = li >= si
    ltri = jnp.where(tril, 1.0, 0.0).astype(BF16)
    utri = jnp.where(li <= si, 1.0, 0.0).astype(BF16)
    low_half = si < SSM_HEAD_DIM
    low_row = lax.broadcasted_iota(jnp.int32, (1, LANES), 1) < SSM_HEAD_DIM
    nt = (((1,), (1,)), ((), ()))
    ri = lax.broadcasted_iota(jnp.int32, (CONV_K * CHUNK, 2 * CHUNK), 0)
    ci = lax.broadcasted_iota(jnp.int32, (CONV_K * CHUNK, 2 * CHUNK), 1)
    shift = jnp.where(ci == CHUNK + (ri & (CHUNK - 1)) - (ri >> 7), 1.0, 0.0).astype(BF16)

    cw = cw_ref[0]
    cb = cb_ref[0]
    a_col = -jnp.exp(acol_ref[0]) * LOG2_E
    dskip = dskip_ref[0]
    nw = nw_ref[...]

    def load(rows):
        return jnp.concatenate([xs_ref[0, rows, :], bm_ref[0, rows, :], cm_ref[0, rows, :]], axis=1)

    def chunk(c, carry):
        r0 = pl.multiple_of(c * CHUNK, CHUNK)
        rows = pl.ds(r0, CHUNK)
        u_prev = load(pl.ds(pl.multiple_of(jnp.maximum(r0 - CHUNK, 0), CHUNK), CHUNK))
        u_prev = jnp.where(c > 0, u_prev, jnp.zeros_like(u_prev))
        taps = jnp.dot(shift, jnp.concatenate([u_prev, load(rows)], axis=0), preferred_element_type=F32)
        acc = cb + cw[CONV_K - 1:CONV_K, :] * taps[0:CHUNK]
        for s in range(1, CONV_K):
            acc = acc + cw[CONV_K - 1 - s:CONV_K - s, :] * taps[s * CHUNK:(s + 1) * CHUNK]
        xbc = _silu(acc)
        x = xbc[:, :GROUP_W]
        bmat = xbc[:, GROUP_W:GROUP_W + D_STATE]
        cmat = xbc[:, GROUP_W + D_STATE:]

        dtt = dtt_ref[0, :, rows]
        dat3 = _split3(dtt * a_col)
        acum_t = sum(jnp.dot(p, utri, preferred_element_type=F32) for p in dat3)
        acum = sum(lax.dot_general(ltri, p, nt, preferred_element_type=F32) for p in dat3)
        w_t = dtt * jnp.exp2(acum_t[:, CHUNK - 1:CHUNK] - acum_t)
        cdec = jnp.exp2(acum[CHUNK - 1:CHUNK, :])

        bt = bmat.T
        cbm = jnp.dot(cmat.astype(BF16), bt.astype(BF16), preferred_element_type=F32)
        x_b = x.astype(BF16)

        y_blocks = []
        for k in range(HEAD_PAIRS):
            cols = slice(k * LANES, (k + 1) * LANES)
            xk = x_b[:, cols]
            zero = jnp.zeros_like(xk)
            x_lo, x_hi = jnp.where(low_half, xk, zero), jnp.where(low_half, zero, xk)
            st = state_ref[:, cols]
            st_b = st.astype(BF16)
            st_lo, st_hi = jnp.where(low_half, st_b, zero), jnp.where(low_half, zero, st_b)
            ms, cs, bts, cds = [], [], [], []
            for r in (2 * k, 2 * k + 1):
                col = jnp.broadcast_to(acum[:, r:r + 1], (CHUNK, CHUNK))
                row = jnp.broadcast_to(acum_t[r:r + 1, :], (CHUNK, CHUNK))
                ldec = jnp.exp2(jnp.where(tril, col - row, -jnp.inf))
                ms.append((cbm * ldec * jnp.broadcast_to(dtt[r:r + 1, :], (CHUNK, CHUNK))).astype(BF16))
                cs.append((cmat * jnp.exp2(col)).astype(BF16))
                bts.append((bt * jnp.broadcast_to(w_t[r:r + 1, :], (CHUNK, CHUNK))).astype(BF16))
                cds.append(jnp.broadcast_to(cdec[:, r:r + 1], (1, LANES)))
            y_blocks.append(jnp.dot(jnp.concatenate(ms + cs, axis=1),
                                    jnp.concatenate([x_lo, x_hi, st_lo, st_hi], axis=0),
                                    preferred_element_type=F32))
            upd = jnp.dot(jnp.concatenate(bts, axis=1), jnp.concatenate([x_lo, x_hi], axis=0),
                          preferred_element_type=F32)
            state_ref[:, cols] = st * jnp.where(low_row, cds[0], cds[1]) + upd
        y = jnp.concatenate(y_blocks, axis=1)

        y = y + dskip * x
        y = y * _silu(z_ref[0, rows, :].astype(F32))
        y = y * lax.rsqrt(jnp.mean(y * y, axis=-1, keepdims=True) + EPS)
        y_ref[0, rows, :] = (y * nw).astype(y_ref.dtype)
        return carry

    lax.fori_loop(0, N_CHUNKS, chunk, 0, unroll=4)


def _ssd(proj, dtt, conv_w_g, conv_b_g, a_col, dskip_x, norm_w):
    view = proj.reshape(BATCH, SEQ, PROJ_W)
    gw, ns = GROUP_W, D_STATE
    blocks = [((SEQ, gw), BF16)] * 3 + [((SEQ, ns), BF16)] * 2 + [((HEADS_PER_SSM_GROUP, SEQ), F32)]
    return pl.pallas_call(
        _ssd_kernel,
        grid=(BATCH, SSM_GROUPS),
        in_specs=[pl.BlockSpec((1, SEQ, gw), lambda b, g: (b, 0, P_XBC // gw + g)),
                  pl.BlockSpec((1, SEQ, ns), lambda b, g: (b, 0, P_B // ns + g)),
                  pl.BlockSpec((1, SEQ, ns), lambda b, g: (b, 0, P_C // ns + g)),
                  pl.BlockSpec((1, SEQ, gw), lambda b, g: (b, 0, P_Z // gw + g)),
                  pl.BlockSpec((1, HEADS_PER_SSM_GROUP, SEQ), lambda b, g: (b, g, 0)),
                  pl.BlockSpec((1, CONV_K, CONV_GROUP_W), lambda b, g: (g, 0, 0)),
                  pl.BlockSpec((1, 1, CONV_GROUP_W), lambda b, g: (g, 0, 0)),
                  pl.BlockSpec((1, HEADS_PER_SSM_GROUP, LANES), lambda b, g: (g, 0, 0)),
                  pl.BlockSpec((1, 1, gw), lambda b, g: (g, 0, 0)),
                  pl.BlockSpec((1, gw), lambda b, g: (0, g))],
        out_specs=pl.BlockSpec((1, SEQ, gw), lambda b, g: (b, 0, g)),
        out_shape=jax.ShapeDtypeStruct((BATCH, SEQ, D_INNER), BF16),
        scratch_shapes=[pltpu.VMEM((ns, gw), F32)],
        compiler_params=_params(("arbitrary", "arbitrary"), blocks, extra=_nbytes((ns, gw), F32) + (6 << 20)),
        name="ssd_scan",
    )(view, view, view, view, dtt, conv_w_g, conv_b_g, a_col, dskip_x, norm_w.reshape(1, D_INNER))


MG_TM = 1024
MG_TN = 512


def _merge_kernel(attn_ref, y_ref, ga_ref, gs_ref, wa_ref, ws_ref, out_ref):
    pa = jnp.dot(attn_ref[...], wa_ref[...], preferred_element_type=F32)
    ps = jnp.dot(y_ref[...], ws_ref[...], preferred_element_type=F32)
    merged = _sigmoid(ga_ref[...].astype(F32)) * pa + _sigmoid(gs_ref[...].astype(F32)) * ps
    out_ref[...] = merged.astype(out_ref.dtype)


def _merge_proj(attn, y, proj, w_attn, w_ssm):
    row = lambda i, j: (i, 0)
    blocks = [((MG_TM, ATTN_OUT), BF16), ((MG_TM, D_INNER), BF16)] + [((MG_TM, MG_TN), BF16)] * 3 \
        + [((ATTN_OUT, MG_TN), BF16), ((D_INNER, MG_TN), BF16)]
    return pl.pallas_call(
        _merge_kernel,
        grid=(TOKENS // MG_TM, D_MODEL // MG_TN),
        in_specs=[pl.BlockSpec((MG_TM, ATTN_OUT), row),
                  pl.BlockSpec((MG_TM, D_INNER), row),
                  pl.BlockSpec((MG_TM, MG_TN), lambda i, j: (i, P_GA // MG_TN + j)),
                  pl.BlockSpec((MG_TM, MG_TN), lambda i, j: (i, P_GS // MG_TN + j)),
                  pl.BlockSpec((ATTN_OUT, MG_TN), lambda i, j: (0, j)),
                  pl.BlockSpec((D_INNER, MG_TN), lambda i, j: (0, j))],
        out_specs=pl.BlockSpec((MG_TM, MG_TN), lambda i, j: (i, j)),
        out_shape=jax.ShapeDtypeStruct((TOKENS, D_MODEL), BF16),
        compiler_params=_params(("arbitrary", "arbitrary"), blocks, extra=4 * _nbytes((MG_TM, MG_TN), F32)),
        name="merge_proj",
    )(attn, y, proj, proj, w_attn, w_ssm)


RS_TM = 1024


def _resid_kernel(a_ref, w_ref, x_ref, g_ref, o_ref):
    o_ref[...] = x_ref[...] + g_ref[0] * jnp.dot(a_ref[...], w_ref[...], preferred_element_type=F32)


def _resid_proj(a, w, x, gate, name):
    k = a.shape[1]
    tn = 1024 if k <= D_MODEL else 512
    blocks = [((RS_TM, k), BF16), ((k, tn), BF16), ((RS_TM, tn), F32), ((RS_TM, tn), F32)]
    return pl.pallas_call(
        _resid_kernel,
        grid=(TOKENS // RS_TM, D_MODEL // tn),
        in_specs=[pl.BlockSpec((RS_TM, k), lambda i, j: (i, 0)),
                  pl.BlockSpec((k, tn), lambda i, j: (0, j)),
                  pl.BlockSpec((RS_TM, tn), lambda i, j: (i, j)),
                  pl.BlockSpec((1, 1, tn), lambda i, j: (i * RS_TM // SEQ, 0, j))],
        out_specs=pl.BlockSpec((RS_TM, tn), lambda i, j: (i, j)),
        out_shape=jax.ShapeDtypeStruct((TOKENS, D_MODEL), F32),
        compiler_params=_params(("arbitrary", "arbitrary"), blocks, extra=_nbytes((RS_TM, tn), F32)),
        name=name,
    )(a, w, x, gate)


FF_TM = 1024
FF_TN = 512


def _swiglu_kernel(h_ref, wg_ref, wu_ref, o_ref):
    h = h_ref[...]
    hg = jnp.dot(h, wg_ref[...], preferred_element_type=F32)
    hu = jnp.dot(h, wu_ref[...], preferred_element_type=F32)
    o_ref[...] = (_silu(hg) * hu).astype(o_ref.dtype)


def _swiglu_in(h, w):
    nblk = D_FF // FF_TN
    blocks = [((FF_TM, D_MODEL), BF16), ((D_MODEL, FF_TN), BF16), ((D_MODEL, FF_TN), BF16), ((FF_TM, FF_TN), BF16)]
    return pl.pallas_call(
        _swiglu_kernel,
        grid=(TOKENS // FF_TM, nblk),
        in_specs=[pl.BlockSpec((FF_TM, D_MODEL), lambda i, j: (i, 0)),
                  pl.BlockSpec((D_MODEL, FF_TN), lambda i, j: (0, j)),
                  pl.BlockSpec((D_MODEL, FF_TN), lambda i, j: (0, nblk + j))],
        out_specs=pl.BlockSpec((FF_TM, FF_TN), lambda i, j: (i, j)),
        out_shape=jax.ShapeDtypeStruct((TOKENS, D_FF), BF16),
        compiler_params=_params(("arbitrary", "arbitrary"), blocks, extra=3 * _nbytes((FF_TM, FF_TN), F32)),
        name="swiglu_in",
    )(h, w, w)


def _mixer(x, h, l, rel_bias, w_in, conv_w, conv_b, dt_bias, a_log, d_skip, ssm_norm_w,
           w_attn_proj, w_ssm_proj, w_out, gate):
    w = w_in[l]
    w_qkv = w[:, :OFF_Z].astype(BF16)
    w_rest = jnp.concatenate([w[:, OFF_Z:OFF_DT], w[:, OFF_GA:]], axis=1).astype(BF16)
    w_dt_t = w[:, OFF_DT:OFF_GA].T.astype(BF16)
    qkv = _qkv_proj(h, w_qkv)
    proj = _matmul(h.reshape(TOKENS, D_MODEL), w_rest, BF16, "in_proj")
    dtt = _dt_proj(h, w_dt_t, dt_bias[l])

    attn = _attention(qkv, rel_bias)

    cw, cbias = conv_w[l], conv_b[l]
    xs_w = cw[:, :D_INNER].reshape(CONV_K, SSM_GROUPS, GROUP_W)
    b_w = cw[:, D_INNER:D_INNER + SSM_GROUPS * D_STATE].reshape(CONV_K, SSM_GROUPS, D_STATE)
    c_w = cw[:, D_INNER + SSM_GROUPS * D_STATE:].reshape(CONV_K, SSM_GROUPS, D_STATE)
    conv_w_g = jnp.concatenate([xs_w, b_w, c_w], axis=2).transpose(1, 0, 2)
    conv_b_g = jnp.concatenate([cbias[:D_INNER].reshape(SSM_GROUPS, GROUP_W),
                                cbias[D_INNER:D_INNER + SSM_GROUPS * D_STATE].reshape(SSM_GROUPS, D_STATE),
                                cbias[D_INNER + SSM_GROUPS * D_STATE:].reshape(SSM_GROUPS, D_STATE)],
                               axis=1).reshape(SSM_GROUPS, 1, CONV_GROUP_W)
    a_col = jnp.broadcast_to(a_log[l].reshape(SSM_GROUPS, HEADS_PER_SSM_GROUP, 1),
                             (SSM_GROUPS, HEADS_PER_SSM_GROUP, LANES))
    dskip_x = jnp.repeat(d_skip[l], SSM_HEAD_DIM).reshape(SSM_GROUPS, 1, GROUP_W)
    y = _ssd(proj, dtt, conv_w_g, conv_b_g, a_col, dskip_x, ssm_norm_w[l])

    merged = _merge_proj(attn.reshape(TOKENS, ATTN_OUT), y.reshape(TOKENS, D_INNER), proj,
                         w_attn_proj[l].astype(BF16), w_ssm_proj[l].astype(BF16))
    return _resid_proj(merged, w_out[l].astype(BF16), x, gate, "out_proj_resid")


def kernel(x, c, rel_bias, norm1_w, norm2_w, w_mod, b_mod, w_in, conv_w, conv_b, dt_bias, a_log, d_skip,
           ssm_norm_w, w_attn_proj, w_ssm_proj, w_out, w_ffn_in, w_ffn_out, final_norm_w):
    mod = _modulation(c, w_mod, b_mod)[:, :BATCH]
    xt = x.reshape(TOKENS, D_MODEL)
    for l in range(DEPTH):
        sh1, sc1, g1, sh2, sc2, g2 = [m.reshape(BATCH, 1, D_MODEL) for m in jnp.split(mod[l], 6, axis=-1)]
        h = _norm(xt.reshape(BATCH, SEQ, D_MODEL), norm1_w[l], sc1, sh1, BF16, True)
        xt = _mixer(xt, h, l, rel_bias, w_in, conv_w, conv_b, dt_bias, a_log, d_skip, ssm_norm_w,
                    w_attn_proj, w_ssm_proj, w_out, g1)
        h = _norm(xt.reshape(BATCH, SEQ, D_MODEL), norm2_w[l], sc2, sh2, BF16, True)
        u = _swiglu_in(h.reshape(TOKENS, D_MODEL), w_ffn_in[l].astype(BF16))
        xt = _resid_proj(u, w_ffn_out[l].astype(BF16), xt, g2, "ffn_out_resid")
    zeros = jnp.zeros((BATCH, 1, D_MODEL), F32)
    return _norm(xt.reshape(BATCH, SEQ, D_MODEL), final_norm_w, zeros, zeros, F32, False)
```

```python
import functools
import math

import jax
import jax.numpy as jnp
import numpy as np
from jax import lax
from jax.experimental import pallas as pl
from jax.experimental.pallas import tpu as pltpu

F32 = jnp.float32
BF16 = jnp.bfloat16

D_MODEL = 2048
BATCH = 4
SEQ = 4096
TOKENS = BATCH * SEQ
DEPTH = 2
DILATED_GROUPS = ((128, 1), (512, 4), (2048, 16))
N_ATTN_GROUPS = len(DILATED_GROUPS)
HEADS_PER_GROUP = 4
N_ATTN_HEADS = N_ATTN_GROUPS * HEADS_PER_GROUP
HEAD_DIM = 128
QKV_W = N_ATTN_HEADS * HEAD_DIM
ATTN_OUT = HEADS_PER_GROUP * HEAD_DIM
ATTN_BLK = 128
REL_BUCKETS = 32
REL_MAX_DIST = 2048
D_INNER = 2 * D_MODEL
SSM_HEAD_DIM = 64
SSM_HEADS = D_INNER // SSM_HEAD_DIM
SSM_GROUPS = 8
HEADS_PER_SSM_GROUP = SSM_HEADS // SSM_GROUPS
GROUP_W = HEADS_PER_SSM_GROUP * SSM_HEAD_DIM
D_STATE = 128
CONV_K = 4
CHUNK = 128
CONV_DIM = D_INNER + 2 * SSM_GROUPS * D_STATE
CONV_GROUP_W = GROUP_W + 2 * D_STATE
D_FF = 5632
SPLITS = (QKV_W, QKV_W, QKV_W, D_INNER, CONV_DIM, SSM_HEADS, D_MODEL, D_MODEL)
OFF_Q, OFF_K, OFF_V, OFF_Z, OFF_XBC, OFF_DT, OFF_GA, OFF_GS = np.cumsum((0,) + SPLITS[:-1]).tolist()
N_QKV_HEADS = 3 * N_ATTN_HEADS
PROJ_W = D_INNER + CONV_DIM
P_Z = 0
P_XBC = D_INNER
P_B = P_XBC + D_INNER
P_C = P_B + SSM_GROUPS * D_STATE
EPS = 1e-6
LOG2_E = math.log2(math.e)

LANES = 128
VMEM_LIMIT_CAP = 60 * 1024 * 1024


def _nbytes(shape, dtype):
    return int(np.prod(shape)) * jnp.dtype(dtype).itemsize


def _params(sem, blocks, extra=0):
    need = 2 * sum(_nbytes(s, d) for s, d in blocks) + extra
    return pltpu.CompilerParams(dimension_semantics=sem,
                                vmem_limit_bytes=min(VMEM_LIMIT_CAP, need + (8 << 20)))


def _silu(v):
    return v / (1.0 + jnp.exp(-v))


def _sigmoid(v):
    return 1.0 / (1.0 + jnp.exp(-v))


MOD_TN = 1024
MOD_ROWS = 8


def _mod_kernel(c_ref, w_ref, b_ref, o_ref):
    ca = _silu(c_ref[...])
    o_ref[0] = jnp.dot(ca, w_ref[0], preferred_element_type=F32) + b_ref[0]


def _modulation(c, w_mod, b_mod):
    c8 = jnp.zeros((MOD_ROWS, D_MODEL), F32).at[:BATCH].set(c)
    n = 6 * D_MODEL
    blocks = [((MOD_ROWS, D_MODEL), F32), ((D_MODEL, MOD_TN), F32), ((1, MOD_TN), F32), ((MOD_ROWS, MOD_TN), F32)]
    return pl.pallas_call(
        _mod_kernel,
        grid=(DEPTH, n // MOD_TN),
        in_specs=[pl.BlockSpec((MOD_ROWS, D_MODEL), lambda l, j: (0, 0)),
                  pl.BlockSpec((1, D_MODEL, MOD_TN), lambda l, j: (l, 0, j)),
                  pl.BlockSpec((1, 1, MOD_TN), lambda l, j: (l, 0, j))],
        out_specs=pl.BlockSpec((1, MOD_ROWS, MOD_TN), lambda l, j: (l, 0, j)),
        out_shape=jax.ShapeDtypeStruct((DEPTH, MOD_ROWS, n), F32),
        compiler_params=_params(("arbitrary", "arbitrary"), blocks),
        name="modulation",
    )(c8, w_mod, b_mod.reshape(DEPTH, 1, n))


NORM_TS = 512


def _norm_kernel(x_ref, w_ref, sc_ref, sh_ref, o_ref, *, modulate):
    x = x_ref[0]
    y = x * lax.rsqrt(jnp.mean(x * x, axis=-1, keepdims=True) + EPS)
    y = y * w_ref[...]
    if modulate:
        y = y * (1.0 + sc_ref[0]) + sh_ref[0]
    o_ref[0] = y.astype(o_ref.dtype)


def _norm(x, w, sc, sh, out_dtype, modulate):
    blocks = [((NORM_TS, D_MODEL), F32), ((NORM_TS, D_MODEL), out_dtype)]
    return pl.pallas_call(
        functools.partial(_norm_kernel, modulate=modulate),
        grid=(BATCH, SEQ // NORM_TS),
        in_specs=[pl.BlockSpec((1, NORM_TS, D_MODEL), lambda b, i: (b, i, 0)),
                  pl.BlockSpec((1, D_MODEL), lambda b, i: (0, 0)),
                  pl.BlockSpec((1, 1, D_MODEL), lambda b, i: (b, 0, 0)),
                  pl.BlockSpec((1, 1, D_MODEL), lambda b, i: (b, 0, 0))],
        out_specs=pl.BlockSpec((1, NORM_TS, D_MODEL), lambda b, i: (b, i, 0)),
        out_shape=jax.ShapeDtypeStruct((BATCH, SEQ, D_MODEL), out_dtype),
        compiler_params=_params(("arbitrary", "arbitrary"), blocks, extra=3 * _nbytes((NORM_TS, D_MODEL), F32)),
        name="rmsnorm_mod" if modulate else "rmsnorm",
    )(x, w.reshape(1, D_MODEL), sc, sh)


def _mm_kernel(a_ref, w_ref, o_ref):
    o_ref[...] = jnp.dot(a_ref[...], w_ref[0], preferred_element_type=F32).astype(o_ref.dtype)


def _matmul(a, w, layer, col0, n, tm, tn, name):
    m, k = a.shape
    blocks = [((tm, k), BF16), ((k, tn), BF16), ((tm, tn), BF16)]
    return pl.pallas_call(
        _mm_kernel,
        grid=(m // tm, n // tn),
        in_specs=[pl.BlockSpec((tm, k), lambda i, j: (i, 0)),
                  pl.BlockSpec((1, k, tn), lambda i, j: (layer, 0, col0 // tn + j))],
        out_specs=pl.BlockSpec((tm, tn), lambda i, j: (i, j)),
        out_shape=jax.ShapeDtypeStruct((m, n), BF16),
        compiler_params=_params(("arbitrary", "arbitrary"), blocks, extra=_nbytes((tm, tn), F32)),
        name=name,
    )(a, w)


QKV_TM = 2048
QKV_HEADS_PER_STEP = 6


def _qkv_kernel(h_ref, w_ref, o_ref):
    res = jnp.dot(h_ref[0], w_ref[0], preferred_element_type=F32)
    for hd in range(QKV_HEADS_PER_STEP):
        o_ref[0, hd] = res[:, hd * HEAD_DIM:(hd + 1) * HEAD_DIM]


def _qkv_proj(h, w, layer):
    tn = QKV_HEADS_PER_STEP * HEAD_DIM
    blocks = [((QKV_TM, D_MODEL), BF16), ((D_MODEL, tn), BF16), ((QKV_TM, tn), F32)]
    return pl.pallas_call(
        _qkv_kernel,
        grid=(BATCH, SEQ // QKV_TM, N_QKV_HEADS // QKV_HEADS_PER_STEP),
        in_specs=[pl.BlockSpec((1, QKV_TM, D_MODEL), lambda b, i, j: (b, i, 0)),
                  pl.BlockSpec((1, D_MODEL, tn), lambda b, i, j: (layer, 0, j))],
        out_specs=pl.BlockSpec((1, QKV_HEADS_PER_STEP, QKV_TM, HEAD_DIM), lambda b, i, j: (b, j, i, 0)),
        out_shape=jax.ShapeDtypeStruct((BATCH, N_QKV_HEADS, SEQ, HEAD_DIM), F32),
        compiler_params=_params(("arbitrary", "arbitrary", "arbitrary"), blocks, extra=_nbytes((QKV_TM, tn), F32)),
        name="qkv_proj",
    )(h, w)


DT_TS = 1024


def _dt_kernel(h_ref, wt_ref, bias_ref, o_ref):
    raw = lax.dot_general(wt_ref[...], h_ref[0], (((1,), (1,)), ((), ())), preferred_element_type=F32)
    v = raw + bias_ref[...]
    o_ref[0] = jnp.maximum(v, 0.0) + jnp.log1p(jnp.exp(-jnp.abs(v)))


def _dt_proj(h, w_dt_t, dt_bias):
    blocks = [((DT_TS, D_MODEL), BF16), ((SSM_HEADS, D_MODEL), BF16), ((SSM_HEADS, DT_TS), F32)]
    return pl.pallas_call(
        _dt_kernel,
        grid=(BATCH, SEQ // DT_TS),
        in_specs=[pl.BlockSpec((1, DT_TS, D_MODEL), lambda b, i: (b, i, 0)),
                  pl.BlockSpec((SSM_HEADS, D_MODEL), lambda b, i: (0, 0)),
                  pl.BlockSpec((SSM_HEADS, 1), lambda b, i: (0, 0))],
        out_specs=pl.BlockSpec((1, SSM_HEADS, DT_TS), lambda b, i: (b, 0, i)),
        out_shape=jax.ShapeDtypeStruct((BATCH, SSM_HEADS, SEQ), F32),
        compiler_params=_params(("arbitrary", "arbitrary"), blocks),
        name="dt_proj",
    )(h, w_dt_t, dt_bias.reshape(SSM_HEADS, 1))


ATTN_SPAN = 2048
SUB_BLOCKS = ATTN_SPAN // ATTN_BLK
MERGE_ROWS = 256


def _bucket_maps():
    qi = np.arange(ATTN_BLK)[:, None]
    kj = np.arange(2 * ATTN_BLK)[None, :]
    steps = np.clip(ATTN_BLK + qi - kj, 0, ATTN_BLK)
    exact = REL_BUCKETS // 2
    maps = []
    for _, dil in DILATED_GROUPS:
        dist = steps * dil
        n = np.maximum(dist, 1).astype(np.float32)
        large = exact + (np.log(n / np.float32(exact)) / np.float32(math.log(REL_MAX_DIST / exact))
                         * np.float32(REL_BUCKETS - exact)).astype(np.int32)
        large = np.minimum(large, REL_BUCKETS - 1)
        maps.append(np.where(dist < exact, dist, large).astype(np.int32))
    return np.stack(maps)


def _attn_kernel(tab_ref, bucket_ref, *refs):
    qkv_refs = refs[:5 * N_ATTN_GROUPS]
    out_ref, o_scr, l_scr, bias_ref, s_scr, p_scr, d_scr = refs[5 * N_ATTN_GROUPS:]
    b, hd, n = pl.program_id(0), pl.program_id(1), pl.program_id(2)

    @pl.when((b == 0) & (n == 0))
    def _():
        for g in range(N_ATTN_GROUPS):
            bucket = bucket_ref[g]
            bias = jnp.zeros((ATTN_BLK, 2 * ATTN_BLK), F32)
            for k in range(REL_BUCKETS):
                bias = jnp.where(bucket == k, tab_ref[k, g * HEADS_PER_GROUP + hd], bias)
            bias_ref[g * HEADS_PER_GROUP + hd] = bias

    kj2 = lax.broadcasted_iota(jnp.int32, (ATTN_BLK, 2 * ATTN_BLK), 1)
    qi2 = lax.broadcasted_iota(jnp.int32, (ATTN_BLK, 2 * ATTN_BLK), 0)
    band = (kj2 >= qi2) & (kj2 <= qi2 + ATTN_BLK)
    band_first = band & ((kj2 >= ATTN_BLK) | (n > 0))
    scale = 1.0 / math.sqrt(HEAD_DIM)
    nt = (((1,), (1,)), ((), ()))

    for g, (_, dil) in enumerate(DILATED_GROUPS):
        q_ref, kc_ref, kp_ref, vc_ref, vp_ref = qkv_refs[5 * g:5 * g + 5]
        bias = bias_ref[g * HEADS_PER_GROUP + hd]
        per_residue = SUB_BLOCKS // dil

        def rows(r, m, dil=dil):
            start = r + m * ATTN_BLK * dil
            return pl.ds(start, ATTN_BLK, stride=dil) if dil > 1 else pl.ds(start, ATTN_BLK)

        def both(prev_ref, cur_ref, r, m):
            prev = prev_ref[0, 0, rows(r, per_residue - 1), :] if m == 0 else cur_ref[0, 0, rows(r, m - 1), :]
            return jnp.concatenate([prev.astype(BF16), cur_ref[0, 0, rows(r, m), :].astype(BF16)], axis=0)

        subs = [(r, m) for r in range(dil) for m in range(per_residue)]
        for i, (r, m) in enumerate(subs):
            q = q_ref[0, 0, rows(r, m), :].astype(BF16)
            s = lax.dot_general(q, both(kp_ref, kc_ref, r, m), nt, preferred_element_type=F32) * scale
            s_scr[i] = jnp.where(band_first if m == 0 else band, s + bias, -jnp.inf)
        for i, (r, m) in enumerate(subs):
            s = s_scr[i]
            mx = jnp.max(s, axis=-1, keepdims=True)
            p = jnp.exp(s - mx)
            den = jnp.sum(p, axis=-1, keepdims=True)
            p_scr[i] = p.astype(BF16)
            d_scr[i] = jnp.broadcast_to(den, (ATTN_BLK, HEAD_DIM))
            l_scr[g, rows(r, m), :] = jnp.broadcast_to(mx + jnp.log(den), (ATTN_BLK, HEAD_DIM))
        for i, (r, m) in enumerate(subs):
            acc = jnp.dot(p_scr[i], both(vp_ref, vc_ref, r, m), preferred_element_type=F32)
            o_scr[g, rows(r, m), :] = acc / d_scr[i]

    for c in range(ATTN_SPAN // MERGE_ROWS):
        sl = pl.ds(c * MERGE_ROWS, MERGE_ROWS)
        l0, l1, l2 = l_scr[0, sl, :], l_scr[1, sl, :], l_scr[2, sl, :]
        mx = jnp.maximum(jnp.maximum(l0, l1), l2)
        e0, e1, e2 = jnp.exp(l0 - mx), jnp.exp(l1 - mx), jnp.exp(l2 - mx)
        den = e0 + e1 + e2
        attn = (e0 / den) * o_scr[0, sl, :] + (e1 / den) * o_scr[1, sl, :] + (e2 / den) * o_scr[2, sl, :]
        out_ref[0, sl, :] = attn.astype(out_ref.dtype)


def _attention(qkv, rel_bias):
    blk = (1, 1, ATTN_SPAN, HEAD_DIM)

    def spec(which, g, prev):
        base = which * N_ATTN_HEADS + g * HEADS_PER_GROUP
        if prev:
            return pl.BlockSpec(blk, lambda b, hd, n: (b, base + hd, jnp.maximum(n - 1, 0), 0))
        return pl.BlockSpec(blk, lambda b, hd, n: (b, base + hd, n, 0))

    qkv_specs = []
    for g in range(N_ATTN_GROUPS):
        qkv_specs += [spec(0, g, False), spec(1, g, False), spec(1, g, True), spec(2, g, False), spec(2, g, True)]
    bias_shape = (N_ATTN_HEADS, ATTN_BLK, 2 * ATTN_BLK)
    scr_shape = (N_ATTN_GROUPS, ATTN_SPAN, HEAD_DIM)
    stage_shape = (SUB_BLOCKS, ATTN_BLK, 2 * ATTN_BLK)
    blocks =[(blk, F32)] * len(qkv_specs) + [((ATTN_SPAN, HEAD_DIM), BF16), (bias_shape, jnp.int32)]
    return pl.pallas_call(
        _attn_kernel,
        grid=(BATCH, HEADS_PER_GROUP, SEQ // ATTN_SPAN),
        in_specs=[pl.BlockSpec(memory_space=pltpu.SMEM),
                  pl.BlockSpec((N_ATTN_GROUPS, ATTN_BLK, 2 * ATTN_BLK), lambda b, hd, n: (0, 0, 0))] + qkv_specs,
        out_specs=pl.BlockSpec((1, ATTN_SPAN, HEAD_DIM), lambda b, hd, n: (b, n, hd)),
        out_shape=jax.ShapeDtypeStruct((BATCH, SEQ, ATTN_OUT), BF16),
        scratch_shapes=[pltpu.VMEM(scr_shape, F32), pltpu.VMEM(scr_shape, F32), pltpu.VMEM(bias_shape, F32),
                        pltpu.VMEM(stage_shape, F32), pltpu.VMEM(stage_shape, BF16),
                        pltpu.VMEM((SUB_BLOCKS, ATTN_BLK, HEAD_DIM), F32)],
        compiler_params=_params(("arbitrary", "arbitrary", "arbitrary"), blocks,
                                extra=2 * _nbytes(scr_shape, F32) + _nbytes(bias_shape, F32)
                                + _nbytes(stage_shape, F32) + _nbytes(stage_shape, BF16)
                                + _nbytes((SUB_BLOCKS, ATTN_BLK, HEAD_DIM), F32)),
        name="dilated_attn",
    )(rel_bias, jnp.asarray(_bucket_maps()), *([qkv] * len(qkv_specs)))


N_CHUNKS = SEQ // CHUNK
HEAD_PAIRS = HEADS_PER_SSM_GROUP // 2


def _split3(v):
    hi = v.astype(BF16)
    r1 = v - hi.astype(F32)
    mid = r1.astype(BF16)
    lo = (r1 - mid.astype(F32)).astype(BF16)
    return hi, mid, lo


def _ssd_kernel(xs_ref, bm_ref, cm_ref, z_ref, dtt_ref, cw_ref, cb_ref, acol_ref, dskip_ref, nw_ref,
                y_ref, state_ref):
    state_ref[...] = jnp.zeros_like(state_ref)

    li = lax.broadcasted_iota(jnp.int32, (CHUNK, CHUNK), 0)
    si = lax.broadcasted_iota(jnp.int32, (CHUNK, CHUNK), 1)
    tril = li >= si
    ltri = jnp.where(tril, 1.0, 0.0).astype(BF16)
    utri = jnp.where(li <= si, 1.0, 0.0).astype(BF16)
    low_half = si < SSM_HEAD_DIM
    low_row = lax.broadcasted_iota(jnp.int32, (1, LANES), 1) < SSM_HEAD_DIM
    nt = (((1,), (1,)), ((), ()))
    ri = lax.broadcasted_iota(jnp.int32, (CONV_K * CHUNK, 2 * CHUNK), 0)
    ci = lax.broadcasted_iota(jnp.int32, (CONV_K * CHUNK, 2 * CHUNK), 1)
    shift = jnp.where(ci == CHUNK + (ri & (CHUNK - 1)) - (ri >> 7), 1.0, 0.0).astype(BF16)

    cw = cw_ref[0]
    cb = cb_ref[0]
    a_col = -jnp.exp(acol_ref[0]) * LOG2_E
    dskip = dskip_ref[0]
    nw = nw_ref[...]

    def load(rows):
        return jnp.concatenate([xs_ref[0, rows, :], bm_ref[0, rows, :], cm_ref[0, rows, :]], axis=1)

    def chunk(c, carry):
        r0 = pl.multiple_of(c * CHUNK, CHUNK)
        rows = pl.ds(r0, CHUNK)
        u_prev = load(pl.ds(pl.multiple_of(jnp.maximum(r0 - CHUNK, 0), CHUNK), CHUNK))
        u_prev = jnp.where(c > 0, u_prev, jnp.zeros_like(u_prev))
        taps = jnp.dot(shift, jnp.concatenate([u_prev, load(rows)], axis=0), preferred_element_type=F32)
        acc = cb + cw[CONV_K - 1:CONV_K, :] * taps[0:CHUNK]
        for s in range(1, CONV_K):
            acc = acc + cw[CONV_K - 1 - s:CONV_K - s, :] * taps[s * CHUNK:(s + 1) * CHUNK]
        xbc = _silu(acc)
        x = xbc[:, :GROUP_W]
        bmat = xbc[:, GROUP_W:GROUP_W + D_STATE]
        cmat = xbc[:, GROUP_W + D_STATE:]

        dtt = dtt_ref[0, :, rows]
        dat3 = _split3(dtt * a_col)
        acum_t = sum(jnp.dot(p, utri, preferred_element_type=F32) for p in dat3)
        acum = sum(lax.dot_general(ltri, p, nt, preferred_element_type=F32) for p in dat3)
        w_t = dtt * jnp.exp2(acum_t[:, CHUNK - 1:CHUNK] - acum_t)
        cdec = jnp.exp2(acum[CHUNK - 1:CHUNK, :])

        bt = bmat.T
        cbm = jnp.dot(cmat.astype(BF16), bt.astype(BF16), preferred_element_type=F32)
        x_b = x.astype(BF16)

        y_blocks = []
        for k in range(HEAD_PAIRS):
            cols = slice(k * LANES, (k + 1) * LANES)
            xk = x_b[:, cols]
            zero = jnp.zeros_like(xk)
            x_lo, x_hi = jnp.where(low_half, xk, zero), jnp.where(low_half, zero, xk)
            st = state_ref[:, cols]
            st_b = st.astype(BF16)
            st_lo, st_hi = jnp.where(low_half, st_b, zero), jnp.where(low_half, zero, st_b)
            ms, cs, bts, cds = [], [], [], []
            for r in (2 * k, 2 * k + 1):
                col = jnp.broadcast_to(acum[:, r:r + 1], (CHUNK, CHUNK))
                row = jnp.broadcast_to(acum_t[r:r + 1, :], (CHUNK, CHUNK))
                ldec = jnp.exp2(jnp.where(tril, col - row, -jnp.inf))
                ms.append((cbm * ldec * jnp.broadcast_to(dtt[r:r + 1, :], (CHUNK, CHUNK))).astype(BF16))
                cs.append((cmat * jnp.exp2(col)).astype(BF16))
                bts.append((bt * jnp.broadcast_to(w_t[r:r + 1, :], (CHUNK, CHUNK))).astype(BF16))
                cds.append(jnp.broadcast_to(cdec[:, r:r + 1], (1, LANES)))
            y_blocks.append(jnp.dot(jnp.concatenate(ms + cs, axis=1),
                                    jnp.concatenate([x_lo, x_hi, st_lo, st_hi], axis=0),
                                    preferred_element_type=F32))
            upd = jnp.dot(jnp.concatenate(bts, axis=1), jnp.concatenate([x_lo, x_hi], axis=0),
                          preferred_element_type=F32)
            state_ref[:, cols] = st * jnp.where(low_row, cds[0], cds[1]) + upd
        y = jnp.concatenate(y_blocks, axis=1)

        y = y + dskip * x
        y = y * _silu(z_ref[0, rows, :].astype(F32))
        y = y * lax.rsqrt(jnp.mean(y * y, axis=-1, keepdims=True) + EPS)
        y_ref[0, rows, :] = (y * nw).astype(y_ref.dtype)
        return carry

    lax.fori_loop(0, N_CHUNKS, chunk, 0, unroll=4)


def _ssd(proj, dtt, conv_w_g, conv_b_g, a_col, dskip_x, norm_w):
    view = proj.reshape(BATCH, SEQ, PROJ_W)
    gw, ns = GROUP_W, D_STATE
    blocks = [((SEQ, gw), BF16)] * 3 + [((SEQ, ns), BF16)] * 2 + [((HEADS_PER_SSM_GROUP, SEQ), F32)]
    return pl.pallas_call(
        _ssd_kernel,
        grid=(BATCH, SSM_GROUPS),
        in_specs=[pl.BlockSpec((1, SEQ, gw), lambda b, g: (b, 0, P_XBC // gw + g)),
                  pl.BlockSpec((1, SEQ, ns), lambda b, g: (b, 0, P_B // ns + g)),
                  pl.BlockSpec((1, SEQ, ns), lambda b, g: (b, 0, P_C // ns + g)),
                  pl.BlockSpec((1, SEQ, gw), lambda b, g: (b, 0, P_Z // gw + g)),
                  pl.BlockSpec((1, HEADS_PER_SSM_GROUP, SEQ), lambda b, g: (b, g, 0)),
                  pl.BlockSpec((1, CONV_K, CONV_GROUP_W), lambda b, g: (g, 0, 0)),
                  pl.BlockSpec((1, 1, CONV_GROUP_W), lambda b, g: (g, 0, 0)),
                  pl.BlockSpec((1, HEADS_PER_SSM_GROUP, LANES), lambda b, g: (g, 0, 0)),
                  pl.BlockSpec((1, 1, gw), lambda b, g: (g, 0, 0)),
                  pl.BlockSpec((1, gw), lambda b, g: (0, g))],
        out_specs=pl.BlockSpec((1, SEQ, gw), lambda b, g: (b, 0, g)),
        out_shape=jax.ShapeDtypeStruct((BATCH, SEQ, D_INNER), BF16),
        scratch_shapes=[pltpu.VMEM((ns, gw), F32)],
        compiler_params=_params(("arbitrary", "arbitrary"), blocks, extra=_nbytes((ns, gw), F32) + (6 << 20)),
        name="ssd_scan",
    )(view, view, view, view, dtt, conv_w_g, conv_b_g, a_col, dskip_x, norm_w.reshape(1, D_INNER))


MG_TM = 1024
MG_TN = 512


def _merge_kernel(attn_ref, y_ref, ga_ref, gs_ref, wa_ref, ws_ref, out_ref):
    pa = jnp.dot(attn_ref[...], wa_ref[0], preferred_element_type=F32)
    ps = jnp.dot(y_ref[...], ws_ref[0], preferred_element_type=F32)
    merged = _sigmoid(ga_ref[...].astype(F32)) * pa + _sigmoid(gs_ref[...].astype(F32)) * ps
    out_ref[...] = merged.astype(out_ref.dtype)


def _merge_proj(attn, y, gates, w_attn, w_ssm, layer):
    row = lambda i, j: (i, 0)
    blocks = [((MG_TM, ATTN_OUT), BF16), ((MG_TM, D_INNER), BF16)] + [((MG_TM, MG_TN), BF16)] * 3 \
        + [((ATTN_OUT, MG_TN), BF16), ((D_INNER, MG_TN), BF16)]
    return pl.pallas_call(
        _merge_kernel,
        grid=(TOKENS // MG_TM, D_MODEL // MG_TN),
        in_specs=[pl.BlockSpec((MG_TM, ATTN_OUT), row),
                  pl.BlockSpec((MG_TM, D_INNER), row),
                  pl.BlockSpec((MG_TM, MG_TN), lambda i, j: (i, j)),
                  pl.BlockSpec((MG_TM, MG_TN), lambda i, j: (i, D_MODEL // MG_TN + j)),
                  pl.BlockSpec((1, ATTN_OUT, MG_TN), lambda i, j: (layer, 0, j)),
                  pl.BlockSpec((1, D_INNER, MG_TN), lambda i, j: (layer, 0, j))],
        out_specs=pl.BlockSpec((MG_TM, MG_TN), lambda i, j: (i, j)),
        out_shape=jax.ShapeDtypeStruct((TOKENS, D_MODEL), BF16),
        compiler_params=_params(("arbitrary", "arbitrary"), blocks, extra=4 * _nbytes((MG_TM, MG_TN), F32)),
        name="merge_proj",
    )(attn, y, gates, gates, w_attn, w_ssm)


RS_TM = 1024


def _resid_kernel(a_ref, w_ref, x_ref, g_ref, o_ref):
    o_ref[...] = x_ref[...] + g_ref[0] * jnp.dot(a_ref[...], w_ref[0], preferred_element_type=F32)


def _resid_proj(a, w, layer, x, gate, name):
    k = a.shape[1]
    tn = 1024 if k <= D_MODEL else 512
    blocks = [((RS_TM, k), BF16), ((k, tn), BF16), ((RS_TM, tn), F32), ((RS_TM, tn), F32)]
    return pl.pallas_call(
        _resid_kernel,
        grid=(TOKENS // RS_TM, D_MODEL // tn),
        in_specs=[pl.BlockSpec((RS_TM, k), lambda i, j: (i, 0)),
                  pl.BlockSpec((1, k, tn), lambda i, j: (layer, 0, j)),
                  pl.BlockSpec((RS_TM, tn), lambda i, j: (i, j)),
                  pl.BlockSpec((1, 1, tn), lambda i, j: (i * RS_TM // SEQ, 0, j))],
        out_specs=pl.BlockSpec((RS_TM, tn), lambda i, j: (i, j)),
        out_shape=jax.ShapeDtypeStruct((TOKENS, D_MODEL), F32),
        compiler_params=_params(("arbitrary", "arbitrary"), blocks, extra=_nbytes((RS_TM, tn), F32)),
        name=name,
    )(a, w, x, gate)


FF_TM = 1024
FF_TN = 512


def _swiglu_kernel(h_ref, wg_ref, wu_ref, o_ref):
    h = h_ref[...]
    hg = jnp.dot(h, wg_ref[0], preferred_element_type=F32)
    hu = jnp.dot(h, wu_ref[0], preferred_element_type=F32)
    o_ref[...] = (_silu(hg) * hu).astype(o_ref.dtype)


def _swiglu_in(h, w, layer):
    nblk = D_FF // FF_TN
    blocks = [((FF_TM, D_MODEL), BF16), ((D_MODEL, FF_TN), BF16), ((D_MODEL, FF_TN), BF16), ((FF_TM, FF_TN), BF16)]
    return pl.pallas_call(
        _swiglu_kernel,
        grid=(TOKENS // FF_TM, nblk),
        in_specs=[pl.BlockSpec((FF_TM, D_MODEL), lambda i, j: (i, 0)),
                  pl.BlockSpec((1, D_MODEL, FF_TN), lambda i, j: (layer, 0, j)),
                  pl.BlockSpec((1, D_MODEL, FF_TN), lambda i, j: (layer, 0, nblk + j))],
        out_specs=pl.BlockSpec((FF_TM, FF_TN), lambda i, j: (i, j)),
        out_shape=jax.ShapeDtypeStruct((TOKENS, D_FF), BF16),
        compiler_params=_params(("arbitrary", "arbitrary"), blocks, extra=3 * _nbytes((FF_TM, FF_TN), F32)),
        name="swiglu_in",
    )(h, w, w)


def _mixer(x, h, l, rel_bias, w_in, w_in_b, conv_w, conv_b, dt_bias, a_log, d_skip, ssm_norm_w,
           w_attn_b, w_ssm_b, w_out_b, gate):
    h2d = h.reshape(TOKENS, D_MODEL)
    qkv = _qkv_proj(h, w_in_b, l)
    proj = _matmul(h2d, w_in_b, l, OFF_Z, PROJ_W, 2048, 512, "in_proj")
    w_gates = w_in_b[l:l + 1, :, OFF_GA:]
    gates = _matmul(h2d, w_gates, 0, 0, 2 * D_MODEL, 1024, 1024, "gate_proj")
    w_dt_t = w_in[l, :, OFF_DT:OFF_GA].T.astype(BF16)
    dtt = _dt_proj(h, w_dt_t, dt_bias[l])

    attn = _attention(qkv, rel_bias)

    cw, cbias = conv_w[l], conv_b[l]
    xs_w = cw[:, :D_INNER].reshape(CONV_K, SSM_GROUPS, GROUP_W)
    b_w = cw[:, D_INNER:D_INNER + SSM_GROUPS * D_STATE].reshape(CONV_K, SSM_GROUPS, D_STATE)
    c_w = cw[:, D_INNER + SSM_GROUPS * D_STATE:].reshape(CONV_K, SSM_GROUPS, D_STATE)
    conv_w_g = jnp.concatenate([xs_w, b_w, c_w], axis=2).transpose(1, 0, 2)
    conv_b_g = jnp.concatenate([cbias[:D_INNER].reshape(SSM_GROUPS, GROUP_W),
                                cbias[D_INNER:D_INNER + SSM_GROUPS * D_STATE].reshape(SSM_GROUPS, D_STATE),
                                cbias[D_INNER + SSM_GROUPS * D_STATE:].reshape(SSM_GROUPS, D_STATE)],
                               axis=1).reshape(SSM_GROUPS, 1, CONV_GROUP_W)
    a_col = jnp.broadcast_to(a_log[l].reshape(SSM_GROUPS, HEADS_PER_SSM_GROUP, 1),
                             (SSM_GROUPS, HEADS_PER_SSM_GROUP, LANES))
    dskip_x = jnp.repeat(d_skip[l], SSM_HEAD_DIM).reshape(SSM_GROUPS, 1, GROUP_W)
    y = _ssd(proj, dtt, conv_w_g, conv_b_g, a_col, dskip_x, ssm_norm_w[l])

    merged = _merge_proj(attn.reshape(TOKENS, ATTN_OUT), y.reshape(TOKENS, D_INNER), gates, w_attn_b, w_ssm_b, l)
    return _resid_proj(merged, w_out_b, l, x, gate, "out_proj_resid")


def kernel(x, c, rel_bias, norm1_w, norm2_w, w_mod, b_mod, w_in, conv_w, conv_b, dt_bias, a_log, d_skip,
           ssm_norm_w, w_attn_proj, w_ssm_proj, w_out, w_ffn_in, w_ffn_out, final_norm_w):
    mod = _modulation(c, w_mod, b_mod)[:, :BATCH]
    xt = x.reshape(TOKENS, D_MODEL)
    w_in_b, w_attn_b, w_ssm_b, w_out_b, w_ffn_in_b, w_ffn_out_b = [
        w.astype(BF16) for w in (w_in, w_attn_proj, w_ssm_proj, w_out, w_ffn_in, w_ffn_out)]
    for l in range(DEPTH):
        sh1, sc1, g1, sh2, sc2, g2 = [m.reshape(BATCH, 1, D_MODEL) for m in jnp.split(mod[l], 6, axis=-1)]
        h = _norm(xt.reshape(BATCH, SEQ, D_MODEL), norm1_w[l], sc1, sh1, BF16, True)
        xt = _mixer(xt, h, l, rel_bias, w_in, w_in_b, conv_w, conv_b, dt_bias, a_log, d_skip, ssm_norm_w,
                    w_attn_b, w_ssm_b, w_out_b, g1)
        h = _norm(xt.reshape(BATCH, SEQ, D_MODEL), norm2_w[l], sc2, sh2, BF16, True)
        u = _swiglu_in(h.reshape(TOKENS, D_MODEL), w_ffn_in_b, l)
        xt = _resid_proj(u, w_ffn_out_b, l, xt, g2, "ffn_out_resid")
    zeros = jnp.zeros((BATCH, 1, D_MODEL), F32)
    return _norm(xt.reshape(BATCH, SEQ, D_MODEL), final_norm_w, zeros, zeros, F32, False)
```

```python
import functools
import math

import jax
import jax.numpy as jnp
import numpy as np
from jax import lax
from jax.experimental import pallas as pl
from jax.experimental.pallas import tpu as pltpu

F32 = jnp.float32
BF16 = jnp.bfloat16

D_MODEL = 2048
BATCH = 4
SEQ = 4096
TOKENS = BATCH * SEQ
DEPTH = 2
DILATED_GROUPS = ((128, 1), (512, 4), (2048, 16))
N_ATTN_GROUPS = len(DILATED_GROUPS)
HEADS_PER_GROUP = 4
N_ATTN_HEADS = N_ATTN_GROUPS * HEADS_PER_GROUP
HEAD_DIM = 128
QKV_W = N_ATTN_HEADS * HEAD_DIM
ATTN_OUT = HEADS_PER_GROUP * HEAD_DIM
ATTN_BLK = 128
REL_BUCKETS = 32
REL_MAX_DIST = 2048
D_INNER = 2 * D_MODEL
SSM_HEAD_DIM = 64
SSM_HEADS = D_INNER // SSM_HEAD_DIM
SSM_GROUPS = 8
HEADS_PER_SSM_GROUP = SSM_HEADS // SSM_GROUPS
GROUP_W = HEADS_PER_SSM_GROUP * SSM_HEAD_DIM
D_STATE = 128
CONV_K = 4
CHUNK = 128
CONV_DIM = D_INNER + 2 * SSM_GROUPS * D_STATE
CONV_GROUP_W = GROUP_W + 2 * D_STATE
D_FF = 5632
SPLITS = (QKV_W, QKV_W, QKV_W, D_INNER, CONV_DIM, SSM_HEADS, D_MODEL, D_MODEL)
OFF_Q, OFF_K, OFF_V, OFF_Z, OFF_XBC, OFF_DT, OFF_GA, OFF_GS = np.cumsum((0,) + SPLITS[:-1]).tolist()
N_QKV_HEADS = 3 * N_ATTN_HEADS
PROJ_W = D_INNER + CONV_DIM
P_Z = 0
P_XBC = D_INNER
P_B = P_XBC + D_INNER
P_C = P_B + SSM_GROUPS * D_STATE
EPS = 1e-6
LOG2_E = math.log2(math.e)

LANES = 128
VMEM_LIMIT_CAP = 60 * 1024 * 1024


def _nbytes(shape, dtype):
    return int(np.prod(shape)) * jnp.dtype(dtype).itemsize


def _params(sem, blocks, extra=0):
    need = 2 * sum(_nbytes(s, d) for s, d in blocks) + extra
    return pltpu.CompilerParams(dimension_semantics=sem,
                                vmem_limit_bytes=min(VMEM_LIMIT_CAP, need + (8 << 20)))


def _sigmoid(v):
    return 0.5 + 0.5 * jnp.tanh(0.5 * v)


def _silu(v):
    h = 0.5 * v
    return h + h * jnp.tanh(h)


MOD_TN = 1024
MOD_ROWS = 8


def _mod_kernel(c_ref, w_ref, b_ref, o_ref):
    ca = _silu(c_ref[...])
    o_ref[0] = jnp.dot(ca, w_ref[0], preferred_element_type=F32) + b_ref[0]


def _modulation(c, w_mod, b_mod):
    c8 = jnp.zeros((MOD_ROWS, D_MODEL), F32).at[:BATCH].set(c)
    n = 6 * D_MODEL
    blocks = [((MOD_ROWS, D_MODEL), F32), ((D_MODEL, MOD_TN), F32), ((1, MOD_TN), F32), ((MOD_ROWS, MOD_TN), F32)]
    return pl.pallas_call(
        _mod_kernel,
        grid=(DEPTH, n // MOD_TN),
        in_specs=[pl.BlockSpec((MOD_ROWS, D_MODEL), lambda l, j: (0, 0)),
                  pl.BlockSpec((1, D_MODEL, MOD_TN), lambda l, j: (l, 0, j)),
                  pl.BlockSpec((1, 1, MOD_TN), lambda l, j: (l, 0, j))],
        out_specs=pl.BlockSpec((1, MOD_ROWS, MOD_TN), lambda l, j: (l, 0, j)),
        out_shape=jax.ShapeDtypeStruct((DEPTH, MOD_ROWS, n), F32),
        compiler_params=_params(("arbitrary", "arbitrary"), blocks),
        name="modulation",
    )(c8, w_mod, b_mod.reshape(DEPTH, 1, n))


NORM_TS = 512


def _norm_kernel(x_ref, w_ref, sc_ref, sh_ref, o_ref, *, modulate):
    x = x_ref[0]
    y = x * lax.rsqrt(jnp.mean(x * x, axis=-1, keepdims=True) + EPS)
    y = y * w_ref[...]
    if modulate:
        y = y * (1.0 + sc_ref[0]) + sh_ref[0]
    o_ref[0] = y.astype(o_ref.dtype)


def _norm(x, w, sc, sh, out_dtype, modulate):
    blocks = [((NORM_TS, D_MODEL), F32), ((NORM_TS, D_MODEL), out_dtype)]
    return pl.pallas_call(
        functools.partial(_norm_kernel, modulate=modulate),
        grid=(BATCH, SEQ // NORM_TS),
        in_specs=[pl.BlockSpec((1, NORM_TS, D_MODEL), lambda b, i: (b, i, 0)),
                  pl.BlockSpec((1, D_MODEL), lambda b, i: (0, 0)),
                  pl.BlockSpec((1, 1, D_MODEL), lambda b, i: (b, 0, 0)),
                  pl.BlockSpec((1, 1, D_MODEL), lambda b, i: (b, 0, 0))],
        out_specs=pl.BlockSpec((1, NORM_TS, D_MODEL), lambda b, i: (b, i, 0)),
        out_shape=jax.ShapeDtypeStruct((BATCH, SEQ, D_MODEL), out_dtype),
        compiler_params=_params(("arbitrary", "arbitrary"), blocks, extra=3 * _nbytes((NORM_TS, D_MODEL), F32)),
        name="rmsnorm_mod" if modulate else "rmsnorm",
    )(x, w.reshape(1, D_MODEL), sc, sh)


def _mm_kernel(a_ref, w_ref, o_ref, *, silu_tiles):
    res = jnp.dot(a_ref[...], w_ref[0], preferred_element_type=F32)
    if silu_tiles:
        res = jnp.where(pl.program_id(1) < silu_tiles, _silu(res), res)
    o_ref[...] = res.astype(o_ref.dtype)


def _matmul(a, w, layer, col0, n, tm, tn, name, silu_cols=0):
    m, k = a.shape
    blocks = [((tm, k), BF16), ((k, tn), BF16), ((tm, tn), BF16)]
    return pl.pallas_call(
        functools.partial(_mm_kernel, silu_tiles=silu_cols // tn),
        grid=(m // tm, n // tn),
        in_specs=[pl.BlockSpec((tm, k), lambda i, j: (i, 0)),
                  pl.BlockSpec((1, k, tn), lambda i, j: (layer, 0, col0 // tn + j))],
        out_specs=pl.BlockSpec((tm, tn), lambda i, j: (i, j)),
        out_shape=jax.ShapeDtypeStruct((m, n), BF16),
        compiler_params=_params(("arbitrary", "arbitrary"), blocks, extra=2 * _nbytes((tm, tn), F32)),
        name=name,
    )(a, w)


QKV_TM = 2048
QKV_HEADS_PER_STEP = 6


def _qkv_kernel(h_ref, w_ref, o_ref):
    res = jnp.dot(h_ref[0], w_ref[0], preferred_element_type=F32)
    for hd in range(QKV_HEADS_PER_STEP):
        o_ref[0, hd] = res[:, hd * HEAD_DIM:(hd + 1) * HEAD_DIM]


def _qkv_proj(h, w, layer):
    tn = QKV_HEADS_PER_STEP * HEAD_DIM
    blocks = [((QKV_TM, D_MODEL), BF16), ((D_MODEL, tn), BF16), ((QKV_TM, tn), F32)]
    return pl.pallas_call(
        _qkv_kernel,
        grid=(BATCH, SEQ // QKV_TM, N_QKV_HEADS // QKV_HEADS_PER_STEP),
        in_specs=[pl.BlockSpec((1, QKV_TM, D_MODEL), lambda b, i, j: (b, i, 0)),
                  pl.BlockSpec((1, D_MODEL, tn), lambda b, i, j: (layer, 0, j))],
        out_specs=pl.BlockSpec((1, QKV_HEADS_PER_STEP, QKV_TM, HEAD_DIM), lambda b, i, j: (b, j, i, 0)),
        out_shape=jax.ShapeDtypeStruct((BATCH, N_QKV_HEADS, SEQ, HEAD_DIM), F32),
        compiler_params=_params(("arbitrary", "arbitrary", "arbitrary"), blocks, extra=_nbytes((QKV_TM, tn), F32)),
        name="qkv_proj",
    )(h, w)


DT_TS = 1024


def _dt_kernel(h_ref, wt_ref, bias_ref, o_ref):
    raw = lax.dot_general(wt_ref[...], h_ref[0], (((1,), (1,)), ((), ())), preferred_element_type=F32)
    v = raw + bias_ref[...]
    o_ref[0] = jnp.maximum(v, 0.0) + jnp.log1p(jnp.exp(-jnp.abs(v)))


def _dt_proj(h, w_dt_t, dt_bias):
    blocks = [((DT_TS, D_MODEL), BF16), ((SSM_HEADS, D_MODEL), BF16), ((SSM_HEADS, DT_TS), F32)]
    return pl.pallas_call(
        _dt_kernel,
        grid=(BATCH, SEQ // DT_TS),
        in_specs=[pl.BlockSpec((1, DT_TS, D_MODEL), lambda b, i: (b, i, 0)),
                  pl.BlockSpec((SSM_HEADS, D_MODEL), lambda b, i: (0, 0)),
                  pl.BlockSpec((SSM_HEADS, 1), lambda b, i: (0, 0))],
        out_specs=pl.BlockSpec((1, SSM_HEADS, DT_TS), lambda b, i: (b, 0, i)),
        out_shape=jax.ShapeDtypeStruct((BATCH, SSM_HEADS, SEQ), F32),
        compiler_params=_params(("arbitrary", "arbitrary"), blocks),
        name="dt_proj",
    )(h, w_dt_t, dt_bias.reshape(SSM_HEADS, 1))


ATTN_SPAN = 2048
SUB_BLOCKS = ATTN_SPAN // ATTN_BLK
MERGE_ROWS = 256


def _bucket_maps():
    qi = np.arange(ATTN_BLK)[:, None]
    kj = np.arange(2 * ATTN_BLK)[None, :]
    steps = np.clip(ATTN_BLK + qi - kj, 0, ATTN_BLK)
    exact = REL_BUCKETS // 2
    maps = []
    for _, dil in DILATED_GROUPS:
        dist = steps * dil
        n = np.maximum(dist, 1).astype(np.float32)
        large = exact + (np.log(n / np.float32(exact)) / np.float32(math.log(REL_MAX_DIST / exact))
                         * np.float32(REL_BUCKETS - exact)).astype(np.int32)
        large = np.minimum(large, REL_BUCKETS - 1)
        maps.append(np.where(dist < exact, dist, large).astype(np.int32))
    return np.stack(maps)


def _attn_kernel(tab_ref, bucket_ref, *refs):
    qkv_refs = refs[:5 * N_ATTN_GROUPS]
    out_ref, o_scr, l_scr, bias_ref, s_scr, p_scr, d_scr = refs[5 * N_ATTN_GROUPS:]
    b, hd, n = pl.program_id(0), pl.program_id(1), pl.program_id(2)

    kj2 = lax.broadcasted_iota(jnp.int32, (ATTN_BLK, 2 * ATTN_BLK), 1)
    qi2 = lax.broadcasted_iota(jnp.int32, (ATTN_BLK, 2 * ATTN_BLK), 0)

    @pl.when((b == 0) & (n == 0))
    def _():
        band = (kj2 >= qi2) & (kj2 <= qi2 + ATTN_BLK)
        for g in range(N_ATTN_GROUPS):
            bucket = bucket_ref[g]
            bias = jnp.zeros((ATTN_BLK, 2 * ATTN_BLK), F32)
            for k in range(REL_BUCKETS):
                bias = jnp.where(bucket == k, tab_ref[k, g * HEADS_PER_GROUP + hd], bias)
            bias_ref[g * HEADS_PER_GROUP + hd] = jnp.where(band, bias * LOG2_E, -jnp.inf)

    has_prev = (kj2 >= ATTN_BLK) | (n > 0)
    scale = LOG2_E / math.sqrt(HEAD_DIM)
    nt = (((1,), (1,)), ((), ()))

    for g, (_, dil) in enumerate(DILATED_GROUPS):
        q_ref, kc_ref, kp_ref, vc_ref, vp_ref = qkv_refs[5 * g:5 * g + 5]
        bias = bias_ref[g * HEADS_PER_GROUP + hd]
        bias_first = jnp.where(has_prev, bias, -jnp.inf)
        per_residue = SUB_BLOCKS // dil

        def rows(r, m, dil=dil):
            start = r + m * ATTN_BLK * dil
            return pl.ds(start, ATTN_BLK, stride=dil) if dil > 1 else pl.ds(start, ATTN_BLK)

        def both(prev_ref, cur_ref, r, m):
            prev = prev_ref[0, 0, rows(r, per_residue - 1), :] if m == 0 else cur_ref[0, 0, rows(r, m - 1), :]
            return jnp.concatenate([prev.astype(BF16), cur_ref[0, 0, rows(r, m), :].astype(BF16)], axis=0)

        subs = [(r, m) for r in range(dil) for m in range(per_residue)]
        for i, (r, m) in enumerate(subs):
            q = q_ref[0, 0, rows(r, m), :].astype(BF16)
            s = lax.dot_general(q, both(kp_ref, kc_ref, r, m), nt, preferred_element_type=F32)
            s_scr[i] = s * scale + (bias_first if m == 0 else bias)
        for i, (r, m) in enumerate(subs):
            s = s_scr[i]
            mx = jnp.max(s, axis=-1, keepdims=True)
            p = jnp.exp2(s - mx)
            den = jnp.sum(p, axis=-1, keepdims=True)
            p_scr[i] = p.astype(BF16)
            d_scr[i] = jnp.broadcast_to(1.0 / den, (ATTN_BLK, HEAD_DIM))
            l_scr[g, rows(r, m), :] = jnp.broadcast_to(mx + jnp.log2(den), (ATTN_BLK, HEAD_DIM))
        for i, (r, m) in enumerate(subs):
            acc = jnp.dot(p_scr[i], both(vp_ref, vc_ref, r, m), preferred_element_type=F32)
            o_scr[g, rows(r, m), :] = acc * d_scr[i]

    for c in range(ATTN_SPAN // MERGE_ROWS):
        sl = pl.ds(c * MERGE_ROWS, MERGE_ROWS)
        l0, l1, l2 = l_scr[0, sl, :], l_scr[1, sl, :], l_scr[2, sl, :]
        mx = jnp.maximum(jnp.maximum(l0, l1), l2)
        e0, e1, e2 = jnp.exp2(l0 - mx), jnp.exp2(l1 - mx), jnp.exp2(l2 - mx)
        attn = (e0 * o_scr[0, sl, :] + e1 * o_scr[1, sl, :] + e2 * o_scr[2, sl, :]) / (e0 + e1 + e2)
        out_ref[0, sl, :] = attn.astype(out_ref.dtype)


def _attention(qkv, rel_bias):
    blk = (1, 1, ATTN_SPAN, HEAD_DIM)

    def spec(which, g, prev):
        base = which * N_ATTN_HEADS + g * HEADS_PER_GROUP
        if prev:
            return pl.BlockSpec(blk, lambda b, hd, n: (b, base + hd, jnp.maximum(n - 1, 0), 0))
        return pl.BlockSpec(blk, lambda b, hd, n: (b, base + hd, n, 0))

    qkv_specs = []
    for g in range(N_ATTN_GROUPS):
        qkv_specs += [spec(0, g, False), spec(1, g, False), spec(1, g, True), spec(2, g, False), spec(2, g, True)]
    bias_shape = (N_ATTN_HEADS, ATTN_BLK, 2 * ATTN_BLK)
    scr_shape = (N_ATTN_GROUPS, ATTN_SPAN, HEAD_DIM)
    stage_shape = (SUB_BLOCKS, ATTN_BLK, 2 * ATTN_BLK)
    blocks =[(blk, F32)] * len(qkv_specs) + [((ATTN_SPAN, HEAD_DIM), BF16), (bias_shape, jnp.int32)]
    return pl.pallas_call(
        _attn_kernel,
        grid=(BATCH, HEADS_PER_GROUP, SEQ // ATTN_SPAN),
        in_specs=[pl.BlockSpec(memory_space=pltpu.SMEM),
                  pl.BlockSpec((N_ATTN_GROUPS, ATTN_BLK, 2 * ATTN_BLK), lambda b, hd, n: (0, 0, 0))] + qkv_specs,
        out_specs=pl.BlockSpec((1, ATTN_SPAN, HEAD_DIM), lambda b, hd, n: (b, n, hd)),
        out_shape=jax.ShapeDtypeStruct((BATCH, SEQ, ATTN_OUT), BF16),
        scratch_shapes=[pltpu.VMEM(scr_shape, F32), pltpu.VMEM(scr_shape, F32), pltpu.VMEM(bias_shape, F32),
                        pltpu.VMEM(stage_shape, F32), pltpu.VMEM(stage_shape, BF16),
                        pltpu.VMEM((SUB_BLOCKS, ATTN_BLK, HEAD_DIM), F32)],
        compiler_params=_params(("arbitrary", "arbitrary", "arbitrary"), blocks,
                                extra=2 * _nbytes(scr_shape, F32) + _nbytes(bias_shape, F32)
                                + _nbytes(stage_shape, F32) + _nbytes(stage_shape, BF16)
                                + _nbytes((SUB_BLOCKS, ATTN_BLK, HEAD_DIM), F32)),
        name="dilated_attn",
    )(rel_bias, jnp.asarray(_bucket_maps()), *([qkv] * len(qkv_specs)))


N_CHUNKS = SEQ // CHUNK
HEAD_PAIRS = HEADS_PER_SSM_GROUP // 2


def _split3(v):
    hi = v.astype(BF16)
    r1 = v - hi.astype(F32)
    mid = r1.astype(BF16)
    lo = (r1 - mid.astype(F32)).astype(BF16)
    return hi, mid, lo


def _ssd_kernel(xs_ref, bm_ref, cm_ref, z_ref, dtt_ref, cw_ref, cb_ref, acol_ref, dskip_ref, nw_ref,
                y_ref, state_ref):
    state_ref[...] = jnp.zeros_like(state_ref)

    li = lax.broadcasted_iota(jnp.int32, (CHUNK, CHUNK), 0)
    si = lax.broadcasted_iota(jnp.int32, (CHUNK, CHUNK), 1)
    tril = li >= si
    ltri = jnp.where(tril, 1.0, 0.0).astype(BF16)
    utri = jnp.where(li <= si, 1.0, 0.0).astype(BF16)
    low_half = si < SSM_HEAD_DIM
    low_row = lax.broadcasted_iota(jnp.int32, (1, LANES), 1) < SSM_HEAD_DIM
    nt = (((1,), (1,)), ((), ()))
    ri = lax.broadcasted_iota(jnp.int32, (CONV_K * CHUNK, 2 * CHUNK), 0)
    ci = lax.broadcasted_iota(jnp.int32, (CONV_K * CHUNK, 2 * CHUNK), 1)
    shift = jnp.where(ci == CHUNK + (ri & (CHUNK - 1)) - (ri >> 7), 1.0, 0.0).astype(BF16)

    cw = cw_ref[0]
    cb = cb_ref[0]
    a_col = -jnp.exp(acol_ref[0]) * LOG2_E
    dskip = dskip_ref[0]
    nw = nw_ref[...]

    def load(rows):
        return jnp.concatenate([xs_ref[0, rows, :], bm_ref[0, rows, :], cm_ref[0, rows, :]], axis=1)

    def chunk(c, carry):
        r0 = pl.multiple_of(c * CHUNK, CHUNK)
        rows = pl.ds(r0, CHUNK)
        u_prev = load(pl.ds(pl.multiple_of(jnp.maximum(r0 - CHUNK, 0), CHUNK), CHUNK))
        u_prev = jnp.where(c > 0, u_prev, jnp.zeros_like(u_prev))
        taps = jnp.dot(shift, jnp.concatenate([u_prev, load(rows)], axis=0), preferred_element_type=F32)
        acc = cb + cw[CONV_K - 1:CONV_K, :] * taps[0:CHUNK]
        for s in range(1, CONV_K):
            acc = acc + cw[CONV_K - 1 - s:CONV_K - s, :] * taps[s * CHUNK:(s + 1) * CHUNK]
        xbc = _silu(acc)
        x = xbc[:, :GROUP_W]
        bmat = xbc[:, GROUP_W:GROUP_W + D_STATE]
        cmat = xbc[:, GROUP_W + D_STATE:]

        dtt = dtt_ref[0, :, rows]
        dat3 = _split3(dtt * a_col)
        acum_t = sum(jnp.dot(p, utri, preferred_element_type=F32) for p in dat3)
        acum = sum(lax.dot_general(ltri, p, nt, preferred_element_type=F32) for p in dat3)
        w_t = dtt * jnp.exp2(acum_t[:, CHUNK - 1:CHUNK] - acum_t)
        cdec = jnp.exp2(acum[CHUNK - 1:CHUNK, :])

        bt = bmat.T
        cbm = jnp.dot(cmat.astype(BF16), bt.astype(BF16), preferred_element_type=F32)
        x_b = x.astype(BF16)

        y_blocks = []
        for k in range(HEAD_PAIRS):
            cols = slice(k * LANES, (k + 1) * LANES)
            xk = x_b[:, cols]
            zero = jnp.zeros_like(xk)
            x_lo, x_hi = jnp.where(low_half, xk, zero), jnp.where(low_half, zero, xk)
            st = state_ref[:, cols]
            st_b = st.astype(BF16)
            st_lo, st_hi = jnp.where(low_half, st_b, zero), jnp.where(low_half, zero, st_b)
            ms, cs, bts, cds = [], [], [], []
            for r in (2 * k, 2 * k + 1):
                col = jnp.broadcast_to(acum[:, r:r + 1], (CHUNK, CHUNK))
                row = jnp.broadcast_to(acum_t[r:r + 1, :], (CHUNK, CHUNK))
                ldec = jnp.exp2(jnp.where(tril, col - row, -jnp.inf))
                ms.append((cbm * ldec * jnp.broadcast_to(dtt[r:r + 1, :], (CHUNK, CHUNK))).astype(BF16))
                cs.append((cmat * jnp.exp2(col)).astype(BF16))
                bts.append((bt * jnp.broadcast_to(w_t[r:r + 1, :], (CHUNK, CHUNK))).astype(BF16))
                cds.append(jnp.broadcast_to(cdec[:, r:r + 1], (1, LANES)))
            y_blocks.append(jnp.dot(jnp.concatenate(ms + cs, axis=1),
                                    jnp.concatenate([x_lo, x_hi, st_lo, st_hi], axis=0),
                                    preferred_element_type=F32))
            upd = jnp.dot(jnp.concatenate(bts, axis=1), jnp.concatenate([x_lo, x_hi], axis=0),
                          preferred_element_type=F32)
            state_ref[:, cols] = st * jnp.where(low_row, cds[0], cds[1]) + upd
        y = jnp.concatenate(y_blocks, axis=1)

        y = y + dskip * x
        y = y * z_ref[0, rows, :].astype(F32)
        y = y * lax.rsqrt(jnp.mean(y * y, axis=-1, keepdims=True) + EPS)
        y_ref[0, rows, :] = (y * nw).astype(y_ref.dtype)
        return carry

    lax.fori_loop(0, N_CHUNKS, chunk, 0, unroll=4)


def _ssd(proj, dtt, conv_w_g, conv_b_g, a_col, dskip_x, norm_w):
    view = proj.reshape(BATCH, SEQ, PROJ_W)
    gw, ns = GROUP_W, D_STATE
    blocks = [((SEQ, gw), BF16)] * 3 + [((SEQ, ns), BF16)] * 2 + [((HEADS_PER_SSM_GROUP, SEQ), F32)]
    return pl.pallas_call(
        _ssd_kernel,
        grid=(BATCH, SSM_GROUPS),
        in_specs=[pl.BlockSpec((1, SEQ, gw), lambda b, g: (b, 0, P_XBC // gw + g)),
                  pl.BlockSpec((1, SEQ, ns), lambda b, g: (b, 0, P_B // ns + g)),
                  pl.BlockSpec((1, SEQ, ns), lambda b, g: (b, 0, P_C // ns + g)),
                  pl.BlockSpec((1, SEQ, gw), lambda b, g: (b, 0, P_Z // gw + g)),
                  pl.BlockSpec((1, HEADS_PER_SSM_GROUP, SEQ), lambda b, g: (b, g, 0)),
                  pl.BlockSpec((1, CONV_K, CONV_GROUP_W), lambda b, g: (g, 0, 0)),
                  pl.BlockSpec((1, 1, CONV_GROUP_W), lambda b, g: (g, 0, 0)),
                  pl.BlockSpec((1, HEADS_PER_SSM_GROUP, LANES), lambda b, g: (g, 0, 0)),
                  pl.BlockSpec((1, 1, gw), lambda b, g: (g, 0, 0)),
                  pl.BlockSpec((1, gw), lambda b, g: (0, g))],
        out_specs=pl.BlockSpec((1, SEQ, gw), lambda b, g: (b, 0, g)),
        out_shape=jax.ShapeDtypeStruct((BATCH, SEQ, D_INNER), BF16),
        scratch_shapes=[pltpu.VMEM((ns, gw), F32)],
        compiler_params=_params(("arbitrary", "arbitrary"), blocks, extra=_nbytes((ns, gw), F32) + (8 << 20)),
        name="ssd_scan",
    )(view, view, view, view, dtt, conv_w_g, conv_b_g, a_col, dskip_x, norm_w.reshape(1, D_INNER))


MG_TM = 1024
MG_TN = 512


def _merge_kernel(attn_ref, y_ref, ga_ref, gs_ref, wa_ref, ws_ref, out_ref):
    pa = jnp.dot(attn_ref[...], wa_ref[0], preferred_element_type=F32)
    ps = jnp.dot(y_ref[...], ws_ref[0], preferred_element_type=F32)
    merged = _sigmoid(ga_ref[...].astype(F32)) * pa + _sigmoid(gs_ref[...].astype(F32)) * ps
    out_ref[...] = merged.astype(out_ref.dtype)


def _merge_proj(attn, y, gates, w_attn, w_ssm, layer):
    row = lambda i, j: (i, 0)
    blocks = [((MG_TM, ATTN_OUT), BF16), ((MG_TM, D_INNER), BF16)] + [((MG_TM, MG_TN), BF16)] * 3 \
        + [((ATTN_OUT, MG_TN), BF16), ((D_INNER, MG_TN), BF16)]
    return pl.pallas_call(
        _merge_kernel,
        grid=(TOKENS // MG_TM, D_MODEL // MG_TN),
        in_specs=[pl.BlockSpec((MG_TM, ATTN_OUT), row),
                  pl.BlockSpec((MG_TM, D_INNER), row),
                  pl.BlockSpec((MG_TM, MG_TN), lambda i, j: (i, j)),
                  pl.BlockSpec((MG_TM, MG_TN), lambda i, j: (i, D_MODEL // MG_TN + j)),
                  pl.BlockSpec((1, ATTN_OUT, MG_TN), lambda i, j: (layer, 0, j)),
                  pl.BlockSpec((1, D_INNER, MG_TN), lambda i, j: (layer, 0, j))],
        out_specs=pl.BlockSpec((MG_TM, MG_TN), lambda i, j: (i, j)),
        out_shape=jax.ShapeDtypeStruct((TOKENS, D_MODEL), BF16),
        compiler_params=_params(("arbitrary", "arbitrary"), blocks, extra=4 * _nbytes((MG_TM, MG_TN), F32)),
        name="merge_proj",
    )(attn, y, gates, gates, w_attn, w_ssm)


RS_TM = 1024


def _resid_kernel(a_ref, w_ref, x_ref, g_ref, o_ref):
    o_ref[...] = x_ref[...] + g_ref[0] * jnp.dot(a_ref[...], w_ref[0], preferred_element_type=F32)


def _resid_proj(a, w, layer, x, gate, name):
    k = a.shape[1]
    tn = 1024 if k <= D_MODEL else 512
    blocks = [((RS_TM, k), BF16), ((k, tn), BF16), ((RS_TM, tn), F32), ((RS_TM, tn), F32)]
    return pl.pallas_call(
        _resid_kernel,
        grid=(TOKENS // RS_TM, D_MODEL // tn),
        in_specs=[pl.BlockSpec((RS_TM, k), lambda i, j: (i, 0)),
                  pl.BlockSpec((1, k, tn), lambda i, j: (layer, 0, j)),
                  pl.BlockSpec((RS_TM, tn), lambda i, j: (i, j)),
                  pl.BlockSpec((1, 1, tn), lambda i, j: (i * RS_TM // SEQ, 0, j))],
        out_specs=pl.BlockSpec((RS_TM, tn), lambda i, j: (i, j)),
        out_shape=jax.ShapeDtypeStruct((TOKENS, D_MODEL), F32),
        compiler_params=_params(("arbitrary", "arbitrary"), blocks, extra=_nbytes((RS_TM, tn), F32)),
        name=name,
    )(a, w, x, gate)


FF_TM = 1024
FF_TN = 512


def _swiglu_kernel(h_ref, wg_ref, wu_ref, o_ref):
    h = h_ref[...]
    hg = jnp.dot(h, wg_ref[0], preferred_element_type=F32)
    hu = jnp.dot(h, wu_ref[0], preferred_element_type=F32)
    o_ref[...] = (_silu(hg) * hu).astype(o_ref.dtype)


def _swiglu_in(h, w, layer):
    nblk = D_FF // FF_TN
    blocks = [((FF_TM, D_MODEL), BF16), ((D_MODEL, FF_TN), BF16), ((D_MODEL, FF_TN), BF16), ((FF_TM, FF_TN), BF16)]
    return pl.pallas_call(
        _swiglu_kernel,
        grid=(TOKENS // FF_TM, nblk),
        in_specs=[pl.BlockSpec((FF_TM, D_MODEL), lambda i, j: (i, 0)),
                  pl.BlockSpec((1, D_MODEL, FF_TN), lambda i, j: (layer, 0, j)),
                  pl.BlockSpec((1, D_MODEL, FF_TN), lambda i, j: (layer, 0, nblk + j))],
        out_specs=pl.BlockSpec((FF_TM, FF_TN), lambda i, j: (i, j)),
        out_shape=jax.ShapeDtypeStruct((TOKENS, D_FF), BF16),
        compiler_params=_params(("arbitrary", "arbitrary"), blocks, extra=3 * _nbytes((FF_TM, FF_TN), F32)),
        name="swiglu_in",
    )(h, w, w)


def _mixer(x, h, l, rel_bias, w_in, w_in_b, conv_w, conv_b, dt_bias, a_log, d_skip, ssm_norm_w,
           w_attn_b, w_ssm_b, w_out_b, gate):
    h2d = h.reshape(TOKENS, D_MODEL)
    qkv = _qkv_proj(h, w_in_b, l)
    proj = _matmul(h2d, w_in_b, l, OFF_Z, PROJ_W, 2048, 512, "in_proj", silu_cols=D_INNER)
    w_gates = w_in_b[l:l + 1, :, OFF_GA:]
    gates = _matmul(h2d, w_gates, 0, 0, 2 * D_MODEL, 1024, 1024, "gate_proj")
    w_dt_t = w_in[l, :, OFF_DT:OFF_GA].T.astype(BF16)
    dtt = _dt_proj(h, w_dt_t, dt_bias[l])

    attn = _attention(qkv, rel_bias)

    cw, cbias = conv_w[l], conv_b[l]
    xs_w = cw[:, :D_INNER].reshape(CONV_K, SSM_GROUPS, GROUP_W)
    b_w = cw[:, D_INNER:D_INNER + SSM_GROUPS * D_STATE].reshape(CONV_K, SSM_GROUPS, D_STATE)
    c_w = cw[:, D_INNER + SSM_GROUPS * D_STATE:].reshape(CONV_K, SSM_GROUPS, D_STATE)
    conv_w_g = jnp.concatenate([xs_w, b_w, c_w], axis=2).transpose(1, 0, 2)
    conv_b_g = jnp.concatenate([cbias[:D_INNER].reshape(SSM_GROUPS, GROUP_W),
                                cbias[D_INNER:D_INNER + SSM_GROUPS * D_STATE].reshape(SSM_GROUPS, D_STATE),
                                cbias[D_INNER + SSM_GROUPS * D_STATE:].reshape(SSM_GROUPS, D_STATE)],
                               axis=1).reshape(SSM_GROUPS, 1, CONV_GROUP_W)
    a_col = jnp.broadcast_to(a_log[l].reshape(SSM_GROUPS, HEADS_PER_SSM_GROUP, 1),
                             (SSM_GROUPS, HEADS_PER_SSM_GROUP, LANES))
    dskip_x = jnp.repeat(d_skip[l], SSM_HEAD_DIM).reshape(SSM_GROUPS, 1, GROUP_W)
    y = _ssd(proj, dtt, conv_w_g, conv_b_g, a_col, dskip_x, ssm_norm_w[l])

    merged = _merge_proj(attn.reshape(TOKENS, ATTN_OUT), y.reshape(TOKENS, D_INNER), gates, w_attn_b, w_ssm_b, l)
    return _resid_proj(merged, w_out_b, l, x, gate, "out_proj_resid")


def kernel(x, c, rel_bias, norm1_w, norm2_w, w_mod, b_mod, w_in, conv_w, conv_b, dt_bias, a_log, d_skip,
           ssm_norm_w, w_attn_proj, w_ssm_proj, w_out, w_ffn_in, w_ffn_out, final_norm_w):
    mod = _modulation(c, w_mod, b_mod)[:, :BATCH]
    xt = x.reshape(TOKENS, D_MODEL)
    w_in_b, w_attn_b, w_ssm_b, w_out_b, w_ffn_in_b, w_ffn_out_b = [
        w.astype(BF16) for w in (w_in, w_attn_proj, w_ssm_proj, w_out, w_ffn_in, w_ffn_out)]
    for l in range(DEPTH):
        sh1, sc1, g1, sh2, sc2, g2 = [m.reshape(BATCH, 1, D_MODEL) for m in jnp.split(mod[l], 6, axis=-1)]
        h = _norm(xt.reshape(BATCH, SEQ, D_MODEL), norm1_w[l], sc1, sh1, BF16, True)
        xt = _mixer(xt, h, l, rel_bias, w_in, w_in_b, conv_w, conv_b, dt_bias, a_log, d_skip, ssm_norm_w,
                    w_attn_b, w_ssm_b, w_out_b, g1)
        h = _norm(xt.reshape(BATCH, SEQ, D_MODEL), norm2_w[l], sc2, sh2, BF16, True)
        u = _swiglu_in(h.reshape(TOKENS, D_MODEL), w_ffn_in_b, l)
        xt = _resid_proj(u, w_ffn_out_b, l, xt, g2, "ffn_out_resid")
    zeros = jnp.zeros((BATCH, 1, D_MODEL), F32)
    return _norm(xt.reshape(BATCH, SEQ, D_MODEL), final_norm_w, zeros, zeros, F32, False)
```

```python
import functools
import math

import jax
import jax.numpy as jnp
import numpy as np
from jax import lax
from jax.experimental import pallas as pl
from jax.experimental.pallas import tpu as pltpu

F32 = jnp.float32
BF16 = jnp.bfloat16

D_MODEL = 2048
BATCH = 4
SEQ = 4096
TOKENS = BATCH * SEQ
DEPTH = 2
DILATED_GROUPS = ((128, 1), (512, 4), (2048, 16))
N_ATTN_GROUPS = len(DILATED_GROUPS)
HEADS_PER_GROUP = 4
N_ATTN_HEADS = N_ATTN_GROUPS * HEADS_PER_GROUP
HEAD_DIM = 128
QKV_W = N_ATTN_HEADS * HEAD_DIM
ATTN_OUT = HEADS_PER_GROUP * HEAD_DIM
ATTN_BLK = 128
REL_BUCKETS = 32
REL_MAX_DIST = 2048
D_INNER = 2 * D_MODEL
SSM_HEAD_DIM = 64
SSM_HEADS = D_INNER // SSM_HEAD_DIM
SSM_GROUPS = 8
HEADS_PER_SSM_GROUP = SSM_HEADS // SSM_GROUPS
GROUP_W = HEADS_PER_SSM_GROUP * SSM_HEAD_DIM
D_STATE = 128
CONV_K = 4
CHUNK = 128
CONV_DIM = D_INNER + 2 * SSM_GROUPS * D_STATE
CONV_GROUP_W = GROUP_W + 2 * D_STATE
D_FF = 5632
SPLITS = (QKV_W, QKV_W, QKV_W, D_INNER, CONV_DIM, SSM_HEADS, D_MODEL, D_MODEL)
OFF_Q, OFF_K, OFF_V, OFF_Z, OFF_XBC, OFF_DT, OFF_GA, OFF_GS = np.cumsum((0,) + SPLITS[:-1]).tolist()
N_QKV_HEADS = 3 * N_ATTN_HEADS
PROJ_W = D_INNER + CONV_DIM
P_Z = 0
P_XBC = D_INNER
P_B = P_XBC + D_INNER
P_C = P_B + SSM_GROUPS * D_STATE
EPS = 1e-6
LOG2_E = math.log2(math.e)

LANES = 128
VMEM_LIMIT_CAP = 60 * 1024 * 1024


def _nbytes(shape, dtype):
    return int(np.prod(shape)) * jnp.dtype(dtype).itemsize


def _params(sem, blocks, extra=0):
    need = 2 * sum(_nbytes(s, d) for s, d in blocks) + extra
    return pltpu.CompilerParams(dimension_semantics=sem,
                                vmem_limit_bytes=min(VMEM_LIMIT_CAP, need + (8 << 20)))


def _sigmoid(v):
    return 0.5 + 0.5 * jnp.tanh(0.5 * v)


def _silu(v):
    h = 0.5 * v
    return h + h * jnp.tanh(h)


MOD_TN = 1024
MOD_ROWS = 8


def _mod_kernel(c_ref, w_ref, b_ref, o_ref):
    ca = _silu(c_ref[...])
    o_ref[0] = jnp.dot(ca, w_ref[0], preferred_element_type=F32) + b_ref[0]


def _modulation(c, w_mod, b_mod):
    c8 = jnp.zeros((MOD_ROWS, D_MODEL), F32).at[:BATCH].set(c)
    n = 6 * D_MODEL
    blocks = [((MOD_ROWS, D_MODEL), F32), ((D_MODEL, MOD_TN), F32), ((1, MOD_TN), F32), ((MOD_ROWS, MOD_TN), F32)]
    return pl.pallas_call(
        _mod_kernel,
        grid=(DEPTH, n // MOD_TN),
        in_specs=[pl.BlockSpec((MOD_ROWS, D_MODEL), lambda l, j: (0, 0)),
                  pl.BlockSpec((1, D_MODEL, MOD_TN), lambda l, j: (l, 0, j)),
                  pl.BlockSpec((1, 1, MOD_TN), lambda l, j: (l, 0, j))],
        out_specs=pl.BlockSpec((1, MOD_ROWS, MOD_TN), lambda l, j: (l, 0, j)),
        out_shape=jax.ShapeDtypeStruct((DEPTH, MOD_ROWS, n), F32),
        compiler_params=_params(("arbitrary", "arbitrary"), blocks),
        name="modulation",
    )(c8, w_mod, b_mod.reshape(DEPTH, 1, n))


NORM_TS = 1024


def _norm_kernel(x_ref, w_ref, sc_ref, sh_ref, o_ref, *, modulate):
    x = x_ref[0]
    y = x * lax.rsqrt(jnp.mean(x * x, axis=-1, keepdims=True) + EPS)
    y = y * w_ref[...]
    if modulate:
        y = y * (1.0 + sc_ref[0]) + sh_ref[0]
    o_ref[0] = y.astype(o_ref.dtype)


def _norm(x, w, sc, sh, out_dtype, modulate):
    blocks = [((NORM_TS, D_MODEL), F32), ((NORM_TS, D_MODEL), out_dtype)]
    return pl.pallas_call(
        functools.partial(_norm_kernel, modulate=modulate),
        grid=(BATCH, SEQ // NORM_TS),
        in_specs=[pl.BlockSpec((1, NORM_TS, D_MODEL), lambda b, i: (b, i, 0)),
                  pl.BlockSpec((1, D_MODEL), lambda b, i: (0, 0)),
                  pl.BlockSpec((1, 1, D_MODEL), lambda b, i: (b, 0, 0)),
                  pl.BlockSpec((1, 1, D_MODEL), lambda b, i: (b, 0, 0))],
        out_specs=pl.BlockSpec((1, NORM_TS, D_MODEL), lambda b, i: (b, i, 0)),
        out_shape=jax.ShapeDtypeStruct((BATCH, SEQ, D_MODEL), out_dtype),
        compiler_params=_params(("arbitrary", "arbitrary"), blocks, extra=3 * _nbytes((NORM_TS, D_MODEL), F32)),
        name="rmsnorm_mod" if modulate else "rmsnorm",
    )(x, w.reshape(1, D_MODEL), sc, sh)


def _mm_kernel(a_ref, w_ref, o_ref):
    o_ref[...] = jnp.dot(a_ref[...], w_ref[0], preferred_element_type=F32).astype(o_ref.dtype)


def _matmul(a, w, layer, col0, n, tm, tn, name):
    m, k = a.shape
    blocks = [((tm, k), BF16), ((k, tn), BF16), ((tm, tn), BF16)]
    return pl.pallas_call(
        _mm_kernel,
        grid=(m // tm, n // tn),
        in_specs=[pl.BlockSpec((tm, k), lambda i, j: (i, 0)),
                  pl.BlockSpec((1, k, tn), lambda i, j: (layer, 0, col0 // tn + j))],
        out_specs=pl.BlockSpec((tm, tn), lambda i, j: (i, j)),
        out_shape=jax.ShapeDtypeStruct((m, n), BF16),
        compiler_params=_params(("arbitrary", "arbitrary"), blocks, extra=_nbytes((tm, tn), F32)),
        name=name,
    )(a, w)


QKV_TM = 2048
QKV_HEADS_PER_STEP = 6


def _qkv_kernel(h_ref, w_ref, o_ref):
    res = jnp.dot(h_ref[0], w_ref[0], preferred_element_type=F32)
    for hd in range(QKV_HEADS_PER_STEP):
        o_ref[0, hd] = res[:, hd * HEAD_DIM:(hd + 1) * HEAD_DIM]


def _qkv_proj(h, w, layer):
    tn = QKV_HEADS_PER_STEP * HEAD_DIM
    blocks = [((QKV_TM, D_MODEL), BF16), ((D_MODEL, tn), BF16), ((QKV_TM, tn), F32)]
    return pl.pallas_call(
        _qkv_kernel,
        grid=(BATCH, SEQ // QKV_TM, N_QKV_HEADS // QKV_HEADS_PER_STEP),
        in_specs=[pl.BlockSpec((1, QKV_TM, D_MODEL), lambda b, i, j: (b, i, 0)),
                  pl.BlockSpec((1, D_MODEL, tn), lambda b, i, j: (layer, 0, j))],
        out_specs=pl.BlockSpec((1, QKV_HEADS_PER_STEP, QKV_TM, HEAD_DIM), lambda b, i, j: (b, j, i, 0)),
        out_shape=jax.ShapeDtypeStruct((BATCH, N_QKV_HEADS, SEQ, HEAD_DIM), F32),
        compiler_params=_params(("arbitrary", "arbitrary", "arbitrary"), blocks, extra=_nbytes((QKV_TM, tn), F32)),
        name="qkv_proj",
    )(h, w)


DT_TS = 1024


def _dt_kernel(h_ref, wt_ref, bias_ref, o_ref):
    raw = lax.dot_general(wt_ref[...], h_ref[0], (((1,), (1,)), ((), ())), preferred_element_type=F32)
    v = raw + bias_ref[...]
    o_ref[0] = jnp.maximum(v, 0.0) + jnp.log1p(jnp.exp(-jnp.abs(v)))


def _dt_proj(h, w_dt_t, dt_bias):
    blocks = [((DT_TS, D_MODEL), BF16), ((SSM_HEADS, D_MODEL), BF16), ((SSM_HEADS, DT_TS), F32)]
    return pl.pallas_call(
        _dt_kernel,
        grid=(BATCH, SEQ // DT_TS),
        in_specs=[pl.BlockSpec((1, DT_TS, D_MODEL), lambda b, i: (b, i, 0)),
                  pl.BlockSpec((SSM_HEADS, D_MODEL), lambda b, i: (0, 0)),
                  pl.BlockSpec((SSM_HEADS, 1), lambda b, i: (0, 0))],
        out_specs=pl.BlockSpec((1, SSM_HEADS, DT_TS), lambda b, i: (b, 0, i)),
        out_shape=jax.ShapeDtypeStruct((BATCH, SSM_HEADS, SEQ), F32),
        compiler_params=_params(("arbitrary", "arbitrary"), blocks),
        name="dt_proj",
    )(h, w_dt_t, dt_bias.reshape(SSM_HEADS, 1))


ATTN_SPAN = 2048
SUB_BLOCKS = ATTN_SPAN // ATTN_BLK
MERGE_ROWS = 256


def _bucket_maps():
    qi = np.arange(ATTN_BLK)[:, None]
    kj = np.arange(2 * ATTN_BLK)[None, :]
    steps = np.clip(ATTN_BLK + qi - kj, 0, ATTN_BLK)
    exact = REL_BUCKETS // 2
    maps = []
    for _, dil in DILATED_GROUPS:
        dist = steps * dil
        n = np.maximum(dist, 1).astype(np.float32)
        large = exact + (np.log(n / np.float32(exact)) / np.float32(math.log(REL_MAX_DIST / exact))
                         * np.float32(REL_BUCKETS - exact)).astype(np.int32)
        large = np.minimum(large, REL_BUCKETS - 1)
        maps.append(np.where(dist < exact, dist, large).astype(np.int32))
    return np.stack(maps)


def _attn_kernel(tab_ref, bucket_ref, *refs):
    qkv_refs = refs[:5 * N_ATTN_GROUPS]
    out_ref, o_scr, l_scr, bias_ref, s_scr, p_scr, d_scr = refs[5 * N_ATTN_GROUPS:]
    b, hd, n = pl.program_id(0), pl.program_id(1), pl.program_id(2)

    kj2 = lax.broadcasted_iota(jnp.int32, (ATTN_BLK, 2 * ATTN_BLK), 1)
    qi2 = lax.broadcasted_iota(jnp.int32, (ATTN_BLK, 2 * ATTN_BLK), 0)

    @pl.when((b == 0) & (n == 0))
    def _():
        band = (kj2 >= qi2) & (kj2 <= qi2 + ATTN_BLK)
        for g in range(N_ATTN_GROUPS):
            bucket = bucket_ref[g]
            bias = jnp.zeros((ATTN_BLK, 2 * ATTN_BLK), F32)
            for k in range(REL_BUCKETS):
                bias = jnp.where(bucket == k, tab_ref[k, g * HEADS_PER_GROUP + hd], bias)
            bias_ref[g * HEADS_PER_GROUP + hd] = jnp.where(band, bias * LOG2_E, -jnp.inf)

    has_prev = (kj2 >= ATTN_BLK) | (n > 0)
    scale = LOG2_E / math.sqrt(HEAD_DIM)
    nt = (((1,), (1,)), ((), ()))

    for g, (_, dil) in enumerate(DILATED_GROUPS):
        q_ref, kc_ref, kp_ref, vc_ref, vp_ref = qkv_refs[5 * g:5 * g + 5]
        bias = bias_ref[g * HEADS_PER_GROUP + hd]
        bias_first = jnp.where(has_prev, bias, -jnp.inf)
        per_residue = SUB_BLOCKS // dil

        def rows(r, m, dil=dil):
            start = r + m * ATTN_BLK * dil
            return pl.ds(start, ATTN_BLK, stride=dil) if dil > 1 else pl.ds(start, ATTN_BLK)

        def both(prev_ref, cur_ref, r, m):
            prev = prev_ref[0, 0, rows(r, per_residue - 1), :] if m == 0 else cur_ref[0, 0, rows(r, m - 1), :]
            return jnp.concatenate([prev.astype(BF16), cur_ref[0, 0, rows(r, m), :].astype(BF16)], axis=0)

        subs = [(r, m) for r in range(dil) for m in range(per_residue)]
        for i, (r, m) in enumerate(subs):
            q = q_ref[0, 0, rows(r, m), :].astype(BF16)
            s = lax.dot_general(q, both(kp_ref, kc_ref, r, m), nt, preferred_element_type=F32)
            s_scr[i] = s * scale + (bias_first if m == 0 else bias)
        for i, (r, m) in enumerate(subs):
            s = s_scr[i]
            mx = jnp.max(s, axis=-1, keepdims=True)
            p = jnp.exp2(s - mx)
            den = jnp.sum(p, axis=-1, keepdims=True)
            p_scr[i] = p.astype(BF16)
            d_scr[i] = jnp.broadcast_to(1.0 / den, (ATTN_BLK, HEAD_DIM))
            l_scr[g, rows(r, m), :] = jnp.broadcast_to(mx + jnp.log2(den), (ATTN_BLK, HEAD_DIM))
        for i, (r, m) in enumerate(subs):
            acc = jnp.dot(p_scr[i], both(vp_ref, vc_ref, r, m), preferred_element_type=F32)
            o_scr[g, rows(r, m), :] = acc * d_scr[i]

    for c in range(ATTN_SPAN // MERGE_ROWS):
        sl = pl.ds(c * MERGE_ROWS, MERGE_ROWS)
        l0, l1, l2 = l_scr[0, sl, :], l_scr[1, sl, :], l_scr[2, sl, :]
        mx = jnp.maximum(jnp.maximum(l0, l1), l2)
        e0, e1, e2 = jnp.exp2(l0 - mx), jnp.exp2(l1 - mx), jnp.exp2(l2 - mx)
        attn = (e0 * o_scr[0, sl, :] + e1 * o_scr[1, sl, :] + e2 * o_scr[2, sl, :]) / (e0 + e1 + e2)
        out_ref[0, sl, :] = attn.astype(out_ref.dtype)


def _attention(qkv, rel_bias):
    blk = (1, 1, ATTN_SPAN, HEAD_DIM)

    def spec(which, g, prev):
        base = which * N_ATTN_HEADS + g * HEADS_PER_GROUP
        if prev:
            return pl.BlockSpec(blk, lambda b, hd, n: (b, base + hd, jnp.maximum(n - 1, 0), 0))
        return pl.BlockSpec(blk, lambda b, hd, n: (b, base + hd, n, 0))

    qkv_specs = []
    for g in range(N_ATTN_GROUPS):
        qkv_specs += [spec(0, g, False), spec(1, g, False), spec(1, g, True), spec(2, g, False), spec(2, g, True)]
    bias_shape = (N_ATTN_HEADS, ATTN_BLK, 2 * ATTN_BLK)
    scr_shape = (N_ATTN_GROUPS, ATTN_SPAN, HEAD_DIM)
    stage_shape = (SUB_BLOCKS, ATTN_BLK, 2 * ATTN_BLK)
    blocks =[(blk, F32)] * len(qkv_specs) + [((ATTN_SPAN, HEAD_DIM), BF16), (bias_shape, jnp.int32)]
    return pl.pallas_call(
        _attn_kernel,
        grid=(BATCH, HEADS_PER_GROUP, SEQ // ATTN_SPAN),
        in_specs=[pl.BlockSpec(memory_space=pltpu.SMEM),
                  pl.BlockSpec((N_ATTN_GROUPS, ATTN_BLK, 2 * ATTN_BLK), lambda b, hd, n: (0, 0, 0))] + qkv_specs,
        out_specs=pl.BlockSpec((1, ATTN_SPAN, HEAD_DIM), lambda b, hd, n: (b, n, hd)),
        out_shape=jax.ShapeDtypeStruct((BATCH, SEQ, ATTN_OUT), BF16),
        scratch_shapes=[pltpu.VMEM(scr_shape, F32), pltpu.VMEM(scr_shape, F32), pltpu.VMEM(bias_shape, F32),
                        pltpu.VMEM(stage_shape, F32), pltpu.VMEM(stage_shape, BF16),
                        pltpu.VMEM((SUB_BLOCKS, ATTN_BLK, HEAD_DIM), F32)],
        compiler_params=_params(("arbitrary", "arbitrary", "arbitrary"), blocks,
                                extra=2 * _nbytes(scr_shape, F32) + _nbytes(bias_shape, F32)
                                + _nbytes(stage_shape, F32) + _nbytes(stage_shape, BF16)
                                + _nbytes((SUB_BLOCKS, ATTN_BLK, HEAD_DIM), F32)),
        name="dilated_attn",
    )(rel_bias, jnp.asarray(_bucket_maps()), *([qkv] * len(qkv_specs)))


N_CHUNKS = SEQ // CHUNK
HEAD_PAIRS = HEADS_PER_SSM_GROUP // 2


def _split3(v):
    hi = v.astype(BF16)
    r1 = v - hi.astype(F32)
    mid = r1.astype(BF16)
    lo = (r1 - mid.astype(F32)).astype(BF16)
    return hi, mid, lo


def _ssd_kernel(xs_ref, bm_ref, cm_ref, z_ref, dtt_ref, cw_ref, cb_ref, acol_ref, dskip_ref, nw_ref,
                y_ref, state_ref):
    state_ref[...] = jnp.zeros_like(state_ref)

    li = lax.broadcasted_iota(jnp.int32, (CHUNK, CHUNK), 0)
    si = lax.broadcasted_iota(jnp.int32, (CHUNK, CHUNK), 1)
    tril = li >= si
    ltri = jnp.where(tril, 1.0, 0.0).astype(BF16)
    utri = jnp.where(li <= si, 1.0, 0.0).astype(BF16)
    low_half = si < SSM_HEAD_DIM
    low_row = lax.broadcasted_iota(jnp.int32, (1, LANES), 1) < SSM_HEAD_DIM
    nt = (((1,), (1,)), ((), ()))
    ri = lax.broadcasted_iota(jnp.int32, (CONV_K * CHUNK, 2 * CHUNK), 0)
    ci = lax.broadcasted_iota(jnp.int32, (CONV_K * CHUNK, 2 * CHUNK), 1)
    shift = jnp.where(ci == CHUNK + (ri & (CHUNK - 1)) - (ri >> 7), 1.0, 0.0).astype(BF16)

    cw = cw_ref[0]
    cb = cb_ref[0]
    a_col = -jnp.exp(acol_ref[0]) * LOG2_E
    dskip = dskip_ref[0]
    nw = nw_ref[...]

    def load(rows):
        return jnp.concatenate([xs_ref[0, rows, :], bm_ref[0, rows, :], cm_ref[0, rows, :]], axis=1)

    def chunk(c, carry):
        r0 = pl.multiple_of(c * CHUNK, CHUNK)
        rows = pl.ds(r0, CHUNK)
        u_prev = load(pl.ds(pl.multiple_of(jnp.maximum(r0 - CHUNK, 0), CHUNK), CHUNK))
        u_prev = jnp.where(c > 0, u_prev, jnp.zeros_like(u_prev))
        taps = jnp.dot(shift, jnp.concatenate([u_prev, load(rows)], axis=0), preferred_element_type=F32)
        acc = cb + cw[CONV_K - 1:CONV_K, :] * taps[0:CHUNK]
        for s in range(1, CONV_K):
            acc = acc + cw[CONV_K - 1 - s:CONV_K - s, :] * taps[s * CHUNK:(s + 1) * CHUNK]
        xbc = _silu(acc)
        x = xbc[:, :GROUP_W]
        bmat = xbc[:, GROUP_W:GROUP_W + D_STATE]
        cmat = xbc[:, GROUP_W + D_STATE:]

        dtt = dtt_ref[0, :, rows]
        dat3 = _split3(dtt * a_col)
        acum_t = sum(jnp.dot(p, utri, preferred_element_type=F32) for p in dat3)
        acum = sum(lax.dot_general(ltri, p, nt, preferred_element_type=F32) for p in dat3)
        w_t = dtt * jnp.exp2(acum_t[:, CHUNK - 1:CHUNK] - acum_t)
        cdec = jnp.exp2(acum[CHUNK - 1:CHUNK, :])

        bt = bmat.T
        cbm = jnp.dot(cmat.astype(BF16), bt.astype(BF16), preferred_element_type=F32)
        x_b = x.astype(BF16)

        y_blocks = []
        for k in range(HEAD_PAIRS):
            cols = slice(k * LANES, (k + 1) * LANES)
            xk = x_b[:, cols]
            zero = jnp.zeros_like(xk)
            x_lo, x_hi = jnp.where(low_half, xk, zero), jnp.where(low_half, zero, xk)
            st = state_ref[:, cols]
            st_b = st.astype(BF16)
            st_lo, st_hi = jnp.where(low_half, st_b, zero), jnp.where(low_half, zero, st_b)
            ms, cs, bts, cds = [], [], [], []
            for r in (2 * k, 2 * k + 1):
                col = jnp.broadcast_to(acum[:, r:r + 1], (CHUNK, CHUNK))
                row = jnp.broadcast_to(acum_t[r:r + 1, :], (CHUNK, CHUNK))
                ldec = jnp.exp2(jnp.where(tril, col - row, -jnp.inf))
                ms.append((cbm * ldec * jnp.broadcast_to(dtt[r:r + 1, :], (CHUNK, CHUNK))).astype(BF16))
                cs.append((cmat * jnp.exp2(col)).astype(BF16))
                bts.append((bt * jnp.broadcast_to(w_t[r:r + 1, :], (CHUNK, CHUNK))).astype(BF16))
                cds.append(jnp.broadcast_to(cdec[:, r:r + 1], (1, LANES)))
            y_blocks.append(jnp.dot(jnp.concatenate(ms + cs, axis=1),
                                    jnp.concatenate([x_lo, x_hi, st_lo, st_hi], axis=0),
                                    preferred_element_type=F32))
            upd = jnp.dot(jnp.concatenate(bts, axis=1), jnp.concatenate([x_lo, x_hi], axis=0),
                          preferred_element_type=F32)
            state_ref[:, cols] = st * jnp.where(low_row, cds[0], cds[1]) + upd
        y = jnp.concatenate(y_blocks, axis=1)

        y = y + dskip * x
        y = y * _silu(z_ref[0, rows, :].astype(F32))
        y = y * lax.rsqrt(jnp.mean(y * y, axis=-1, keepdims=True) + EPS)
        y_ref[0, rows, :] = (y * nw).astype(y_ref.dtype)
        return carry

    lax.fori_loop(0, N_CHUNKS, chunk, 0, unroll=4)


def _ssd(proj, dtt, conv_w_g, conv_b_g, a_col, dskip_x, norm_w):
    view = proj.reshape(BATCH, SEQ, PROJ_W)
    gw, ns = GROUP_W, D_STATE
    blocks = [((SEQ, gw), BF16)] * 3 + [((SEQ, ns), BF16)] * 2 + [((HEADS_PER_SSM_GROUP, SEQ), F32)]
    return pl.pallas_call(
        _ssd_kernel,
        grid=(BATCH, SSM_GROUPS),
        in_specs=[pl.BlockSpec((1, SEQ, gw), lambda b, g: (b, 0, P_XBC // gw + g)),
                  pl.BlockSpec((1, SEQ, ns), lambda b, g: (b, 0, P_B // ns + g)),
                  pl.BlockSpec((1, SEQ, ns), lambda b, g: (b, 0, P_C // ns + g)),
                  pl.BlockSpec((1, SEQ, gw), lambda b, g: (b, 0, P_Z // gw + g)),
                  pl.BlockSpec((1, HEADS_PER_SSM_GROUP, SEQ), lambda b, g: (b, g, 0)),
                  pl.BlockSpec((1, CONV_K, CONV_GROUP_W), lambda b, g: (g, 0, 0)),
                  pl.BlockSpec((1, 1, CONV_GROUP_W), lambda b, g: (g, 0, 0)),
                  pl.BlockSpec((1, HEADS_PER_SSM_GROUP, LANES), lambda b, g: (g, 0, 0)),
                  pl.BlockSpec((1, 1, gw), lambda b, g: (g, 0, 0)),
                  pl.BlockSpec((1, gw), lambda b, g: (0, g))],
        out_specs=pl.BlockSpec((1, SEQ, gw), lambda b, g: (b, 0, g)),
        out_shape=jax.ShapeDtypeStruct((BATCH, SEQ, D_INNER), BF16),
        scratch_shapes=[pltpu.VMEM((ns, gw), F32)],
        compiler_params=_params(("arbitrary", "arbitrary"), blocks, extra=_nbytes((ns, gw), F32) + (8 << 20)),
        name="ssd_scan",
    )(view, view, view, view, dtt, conv_w_g, conv_b_g, a_col, dskip_x, norm_w.reshape(1, D_INNER))


MG_TM = 1024
MG_TN = 512


def _merge_kernel(attn_ref, y_ref, ga_ref, gs_ref, wa_ref, ws_ref, out_ref):
    pa = jnp.dot(attn_ref[...], wa_ref[0], preferred_element_type=F32)
    ps = jnp.dot(y_ref[...], ws_ref[0], preferred_element_type=F32)
    merged = _sigmoid(ga_ref[...].astype(F32)) * pa + _sigmoid(gs_ref[...].astype(F32)) * ps
    out_ref[...] = merged.astype(out_ref.dtype)


def _merge_proj(attn, y, gates, w_attn, w_ssm, layer):
    row = lambda i, j: (i, 0)
    blocks = [((MG_TM, ATTN_OUT), BF16), ((MG_TM, D_INNER), BF16)] + [((MG_TM, MG_TN), BF16)] * 3 \
        + [((ATTN_OUT, MG_TN), BF16), ((D_INNER, MG_TN), BF16)]
    return pl.pallas_call(
        _merge_kernel,
        grid=(TOKENS // MG_TM, D_MODEL // MG_TN),
        in_specs=[pl.BlockSpec((MG_TM, ATTN_OUT), row),
                  pl.BlockSpec((MG_TM, D_INNER), row),
                  pl.BlockSpec((MG_TM, MG_TN), lambda i, j: (i, j)),
                  pl.BlockSpec((MG_TM, MG_TN), lambda i, j: (i, D_MODEL // MG_TN + j)),
                  pl.BlockSpec((1, ATTN_OUT, MG_TN), lambda i, j: (layer, 0, j)),
                  pl.BlockSpec((1, D_INNER, MG_TN), lambda i, j: (layer, 0, j))],
        out_specs=pl.BlockSpec((MG_TM, MG_TN), lambda i, j: (i, j)),
        out_shape=jax.ShapeDtypeStruct((TOKENS, D_MODEL), BF16),
        compiler_params=_params(("arbitrary", "arbitrary"), blocks, extra=4 * _nbytes((MG_TM, MG_TN), F32)),
        name="merge_proj",
    )(attn, y, gates, gates, w_attn, w_ssm)


def _resid_kernel(a_ref, w_ref, x_ref, g_ref, o_ref):
    o_ref[...] = x_ref[...] + g_ref[0] * jnp.dot(a_ref[...], w_ref[0], preferred_element_type=F32)


def _resid_proj(a, w, layer, x, gate, name):
    k = a.shape[1]
    tm, tn = (512, D_MODEL) if k <= D_MODEL else (1024, 512)
    blocks = [((tm, k), BF16), ((k, tn), BF16), ((tm, tn), F32), ((tm, tn), F32)]
    return pl.pallas_call(
        _resid_kernel,
        grid=(TOKENS // tm, D_MODEL // tn),
        in_specs=[pl.BlockSpec((tm, k), lambda i, j: (i, 0)),
                  pl.BlockSpec((1, k, tn), lambda i, j: (layer, 0, j)),
                  pl.BlockSpec((tm, tn), lambda i, j: (i, j)),
                  pl.BlockSpec((1, 1, tn), lambda i, j: (i * tm // SEQ, 0, j))],
        out_specs=pl.BlockSpec((tm, tn), lambda i, j: (i, j)),
        out_shape=jax.ShapeDtypeStruct((TOKENS, D_MODEL), F32),
        compiler_params=_params(("arbitrary", "arbitrary"), blocks, extra=_nbytes((tm, tn), F32)),
        name=name,
    )(a, w, x, gate)


FF_TM = 1024
FF_TN = 512


def _swiglu_kernel(h_ref, wg_ref, wu_ref, o_ref):
    h = h_ref[...]
    hg = jnp.dot(h, wg_ref[0], preferred_element_type=F32)
    hu = jnp.dot(h, wu_ref[0], preferred_element_type=F32)
    o_ref[...] = (_silu(hg) * hu).astype(o_ref.dtype)


def _swiglu_in(h, w, layer):
    nblk = D_FF // FF_TN
    blocks = [((FF_TM, D_MODEL), BF16), ((D_MODEL, FF_TN), BF16), ((D_MODEL, FF_TN), BF16), ((FF_TM, FF_TN), BF16)]
    return pl.pallas_call(
        _swiglu_kernel,
        grid=(TOKENS // FF_TM, nblk),
        in_specs=[pl.BlockSpec((FF_TM, D_MODEL), lambda i, j: (i, 0)),
                  pl.BlockSpec((1, D_MODEL, FF_TN), lambda i, j: (layer, 0, j)),
                  pl.BlockSpec((1, D_MODEL, FF_TN), lambda i, j: (layer, 0, nblk + j))],
        out_specs=pl.BlockSpec((FF_TM, FF_TN), lambda i, j: (i, j)),
        out_shape=jax.ShapeDtypeStruct((TOKENS, D_FF), BF16),
        compiler_params=_params(("arbitrary", "arbitrary"), blocks, extra=3 * _nbytes((FF_TM, FF_TN), F32)),
        name="swiglu_in",
    )(h, w, w)


def _mixer(x, h, l, rel_bias, w_in, w_in_b, conv_w, conv_b, dt_bias, a_log, d_skip, ssm_norm_w,
           w_attn_b, w_ssm_b, w_out_b, gate):
    h2d = h.reshape(TOKENS, D_MODEL)
    qkv = _qkv_proj(h, w_in_b, l)
    proj = _matmul(h2d, w_in_b, l, OFF_Z, PROJ_W, 2048, 512, "in_proj")
    w_gates = w_in_b[l:l + 1, :, OFF_GA:]
    gates = _matmul(h2d, w_gates, 0, 0, 2 * D_MODEL, 1024, 1024, "gate_proj")
    w_dt_t = w_in[l, :, OFF_DT:OFF_GA].T.astype(BF16)
    dtt = _dt_proj(h, w_dt_t, dt_bias[l])

    attn = _attention(qkv, rel_bias)

    cw, cbias = conv_w[l], conv_b[l]
    xs_w = cw[:, :D_INNER].reshape(CONV_K, SSM_GROUPS, GROUP_W)
    b_w = cw[:, D_INNER:D_INNER + SSM_GROUPS * D_STATE].reshape(CONV_K, SSM_GROUPS, D_STATE)
    c_w = cw[:, D_INNER + SSM_GROUPS * D_STATE:].reshape(CONV_K, SSM_GROUPS, D_STATE)
    conv_w_g = jnp.concatenate([xs_w, b_w, c_w], axis=2).transpose(1, 0, 2)
    conv_b_g = jnp.concatenate([cbias[:D_INNER].reshape(SSM_GROUPS, GROUP_W),
                                cbias[D_INNER:D_INNER + SSM_GROUPS * D_STATE].reshape(SSM_GROUPS, D_STATE),
                                cbias[D_INNER + SSM_GROUPS * D_STATE:].reshape(SSM_GROUPS, D_STATE)],
                               axis=1).reshape(SSM_GROUPS, 1, CONV_GROUP_W)
    a_col = jnp.broadcast_to(a_log[l].reshape(SSM_GROUPS, HEADS_PER_SSM_GROUP, 1),
                             (SSM_GROUPS, HEADS_PER_SSM_GROUP, LANES))
    dskip_x = jnp.repeat(d_skip[l], SSM_HEAD_DIM).reshape(SSM_GROUPS, 1, GROUP_W)
    y = _ssd(proj, dtt, conv_w_g, conv_b_g, a_col, dskip_x, ssm_norm_w[l])

    merged = _merge_proj(attn.reshape(TOKENS, ATTN_OUT), y.reshape(TOKENS, D_INNER), gates, w_attn_b, w_ssm_b, l)
    return _resid_proj(merged, w_out_b, l, x, gate, "out_proj_resid")


def kernel(x, c, rel_bias, norm1_w, norm2_w, w_mod, b_mod, w_in, conv_w, conv_b, dt_bias, a_log, d_skip,
           ssm_norm_w, w_attn_proj, w_ssm_proj, w_out, w_ffn_in, w_ffn_out, final_norm_w):
    mod = _modulation(c, w_mod, b_mod)[:, :BATCH]
    xt = x.reshape(TOKENS, D_MODEL)
    w_in_b, w_attn_b, w_ssm_b, w_out_b, w_ffn_in_b, w_ffn_out_b = [
        w.astype(BF16) for w in (w_in, w_attn_proj, w_ssm_proj, w_out, w_ffn_in, w_ffn_out)]
    for l in range(DEPTH):
        sh1, sc1, g1, sh2, sc2, g2 = [m.reshape(BATCH, 1, D_MODEL) for m in jnp.split(mod[l], 6, axis=-1)]
        h = _norm(xt.reshape(BATCH, SEQ, D_MODEL), norm1_w[l], sc1, sh1, BF16, True)
        xt = _mixer(xt, h, l, rel_bias, w_in, w_in_b, conv_w, conv_b, dt_bias, a_log, d_skip, ssm_norm_w,
                    w_attn_b, w_ssm_b, w_out_b, g1)
        h = _norm(xt.reshape(BATCH, SEQ, D_MODEL), norm2_w[l], sc2, sh2, BF16, True)
        u = _swiglu_in(h.reshape(TOKENS, D_MODEL), w_ffn_in_b, l)
        xt = _resid_proj(u, w_ffn_out_b, l, xt, g2, "ffn_out_resid")
    zeros = jnp.zeros((BATCH, 1, D_MODEL), F32)
    return _norm(xt.reshape(BATCH, SEQ, D_MODEL), final_norm_w, zeros, zeros, F32, False)
```

```python
import functools
import math

import jax
import jax.numpy as jnp
import numpy as np
from jax import lax
from jax.experimental import pallas as pl
from jax.experimental.pallas import tpu as pltpu

F32 = jnp.float32
BF16 = jnp.bfloat16

D_MODEL = 2048
BATCH = 4
SEQ = 4096
TOKENS = BATCH * SEQ
DEPTH = 2
DILATED_GROUPS = ((128, 1), (512, 4), (2048, 16))
N_ATTN_GROUPS = len(DILATED_GROUPS)
HEADS_PER_GROUP = 4
N_ATTN_HEADS = N_ATTN_GROUPS * HEADS_PER_GROUP
HEAD_DIM = 128
QKV_W = N_ATTN_HEADS * HEAD_DIM
ATTN_OUT = HEADS_PER_GROUP * HEAD_DIM
ATTN_BLK = 128
REL_BUCKETS = 32
REL_MAX_DIST = 2048
D_INNER = 2 * D_MODEL
SSM_HEAD_DIM = 64
SSM_HEADS = D_INNER // SSM_HEAD_DIM
SSM_GROUPS = 8
HEADS_PER_SSM_GROUP = SSM_HEADS // SSM_GROUPS
GROUP_W = HEADS_PER_SSM_GROUP * SSM_HEAD_DIM
D_STATE = 128
CONV_K = 4
CHUNK = 128
CONV_DIM = D_INNER + 2 * SSM_GROUPS * D_STATE
CONV_GROUP_W = GROUP_W + 2 * D_STATE
D_FF = 5632
SPLITS = (QKV_W, QKV_W, QKV_W, D_INNER, CONV_DIM, SSM_HEADS, D_MODEL, D_MODEL)
OFF_Q, OFF_K, OFF_V, OFF_Z, OFF_XBC, OFF_DT, OFF_GA, OFF_GS = np.cumsum((0,) + SPLITS[:-1]).tolist()
N_QKV_HEADS = 3 * N_ATTN_HEADS
PROJ_W = D_INNER + CONV_DIM
P_Z = 0
P_XBC = D_INNER
P_B = P_XBC + D_INNER
P_C = P_B + SSM_GROUPS * D_STATE
EPS = 1e-6
LOG2_E = math.log2(math.e)

LANES = 128
VMEM_LIMIT_CAP = 60 * 1024 * 1024


def _nbytes(shape, dtype):
    return int(np.prod(shape)) * jnp.dtype(dtype).itemsize


def _params(sem, blocks, extra=0):
    need = 2 * sum(_nbytes(s, d) for s, d in blocks) + extra
    return pltpu.CompilerParams(dimension_semantics=sem,
                                vmem_limit_bytes=min(VMEM_LIMIT_CAP, need + (8 << 20)))


def _sigmoid(v):
    return 0.5 + 0.5 * jnp.tanh(0.5 * v)


def _silu(v):
    h = 0.5 * v
    return h + h * jnp.tanh(h)


CAST_BLOCK_BYTES = 8 << 20


def _cast_kernel(w_ref, o_ref):
    o_ref[...] = w_ref[...].astype(o_ref.dtype)


def _to_bf16(w):
    depth, k, n = w.shape
    tk = k
    while tk * n * 4 > CAST_BLOCK_BYTES and tk % 32 == 0:
        tk //= 2
    blocks = [((tk, n), F32), ((tk, n), BF16)]
    return pl.pallas_call(
        _cast_kernel,
        grid=(depth, k // tk),
        in_specs=[pl.BlockSpec((1, tk, n), lambda l, i: (l, i, 0))],
        out_specs=pl.BlockSpec((1, tk, n), lambda l, i: (l, i, 0)),
        out_shape=jax.ShapeDtypeStruct(w.shape, BF16),
        compiler_params=_params(("arbitrary", "arbitrary"), blocks),
        name="weight_to_bf16",
    )(w)


MOD_TN = 1024
MOD_ROWS = 8


def _mod_kernel(c_ref, w_ref, b_ref, o_ref):
    ca = _silu(c_ref[...])
    o_ref[0] = jnp.dot(ca, w_ref[0], preferred_element_type=F32) + b_ref[0]


def _modulation(c, w_mod, b_mod):
    c8 = jnp.zeros((MOD_ROWS, D_MODEL), F32).at[:BATCH].set(c)
    n = 6 * D_MODEL
    blocks = [((MOD_ROWS, D_MODEL), F32), ((D_MODEL, MOD_TN), F32), ((1, MOD_TN), F32), ((MOD_ROWS, MOD_TN), F32)]
    return pl.pallas_call(
        _mod_kernel,
        grid=(DEPTH, n // MOD_TN),
        in_specs=[pl.BlockSpec((MOD_ROWS, D_MODEL), lambda l, j: (0, 0)),
                  pl.BlockSpec((1, D_MODEL, MOD_TN), lambda l, j: (l, 0, j)),
                  pl.BlockSpec((1, 1, MOD_TN), lambda l, j: (l, 0, j))],
        out_specs=pl.BlockSpec((1, MOD_ROWS, MOD_TN), lambda l, j: (l, 0, j)),
        out_shape=jax.ShapeDtypeStruct((DEPTH, MOD_ROWS, n), F32),
        compiler_params=_params(("arbitrary", "arbitrary"), blocks),
        name="modulation",
    )(c8, w_mod, b_mod.reshape(DEPTH, 1, n))


NORM_TS = 1024


def _norm_kernel(x_ref, w_ref, sc_ref, sh_ref, o_ref, *, modulate):
    x = x_ref[0]
    y = x * lax.rsqrt(jnp.mean(x * x, axis=-1, keepdims=True) + EPS)
    y = y * w_ref[...]
    if modulate:
        y = y * (1.0 + sc_ref[0]) + sh_ref[0]
    o_ref[0] = y.astype(o_ref.dtype)


def _norm(x, w, sc, sh, out_dtype, modulate):
    blocks = [((NORM_TS, D_MODEL), F32), ((NORM_TS, D_MODEL), out_dtype)]
    return pl.pallas_call(
        functools.partial(_norm_kernel, modulate=modulate),
        grid=(BATCH, SEQ // NORM_TS),
        in_specs=[pl.BlockSpec((1, NORM_TS, D_MODEL), lambda b, i: (b, i, 0)),
                  pl.BlockSpec((1, D_MODEL), lambda b, i: (0, 0)),
                  pl.BlockSpec((1, 1, D_MODEL), lambda b, i: (b, 0, 0)),
                  pl.BlockSpec((1, 1, D_MODEL), lambda b, i: (b, 0, 0))],
        out_specs=pl.BlockSpec((1, NORM_TS, D_MODEL), lambda b, i: (b, i, 0)),
        out_shape=jax.ShapeDtypeStruct((BATCH, SEQ, D_MODEL), out_dtype),
        compiler_params=_params(("arbitrary", "arbitrary"), blocks, extra=3 * _nbytes((NORM_TS, D_MODEL), F32)),
        name="rmsnorm_mod" if modulate else "rmsnorm",
    )(x, w.reshape(1, D_MODEL), sc, sh)


def _mm_kernel(a_ref, w_ref, o_ref):
    o_ref[...] = jnp.dot(a_ref[...], w_ref[0], preferred_element_type=F32).astype(o_ref.dtype)


def _matmul(a, w, layer, col0, n, tm, tn, name):
    m, k = a.shape
    blocks = [((tm, k), BF16), ((k, tn), BF16), ((tm, tn), BF16)]
    return pl.pallas_call(
        _mm_kernel,
        grid=(m // tm, n // tn),
        in_specs=[pl.BlockSpec((tm, k), lambda i, j: (i, 0)),
                  pl.BlockSpec((1, k, tn), lambda i, j: (layer, 0, col0 // tn + j))],
        out_specs=pl.BlockSpec((tm, tn), lambda i, j: (i, j)),
        out_shape=jax.ShapeDtypeStruct((m, n), BF16),
        compiler_params=_params(("arbitrary", "arbitrary"), blocks, extra=_nbytes((tm, tn), F32)),
        name=name,
    )(a, w)


QKV_TM = 2048
QKV_HEADS_PER_STEP = 6


def _qkv_kernel(h_ref, w_ref, o_ref):
    res = jnp.dot(h_ref[0], w_ref[0], preferred_element_type=F32)
    for hd in range(QKV_HEADS_PER_STEP):
        o_ref[0, hd] = res[:, hd * HEAD_DIM:(hd + 1) * HEAD_DIM]


def _qkv_proj(h, w, layer):
    tn = QKV_HEADS_PER_STEP * HEAD_DIM
    blocks = [((QKV_TM, D_MODEL), BF16), ((D_MODEL, tn), BF16), ((QKV_TM, tn), F32)]
    return pl.pallas_call(
        _qkv_kernel,
        grid=(BATCH, SEQ // QKV_TM, N_QKV_HEADS // QKV_HEADS_PER_STEP),
        in_specs=[pl.BlockSpec((1, QKV_TM, D_MODEL), lambda b, i, j: (b, i, 0)),
                  pl.BlockSpec((1, D_MODEL, tn), lambda b, i, j: (layer, 0, j))],
        out_specs=pl.BlockSpec((1, QKV_HEADS_PER_STEP, QKV_TM, HEAD_DIM), lambda b, i, j: (b, j, i, 0)),
        out_shape=jax.ShapeDtypeStruct((BATCH, N_QKV_HEADS, SEQ, HEAD_DIM), F32),
        compiler_params=_params(("arbitrary", "arbitrary", "arbitrary"), blocks, extra=_nbytes((QKV_TM, tn), F32)),
        name="qkv_proj",
    )(h, w)


DT_TS = 1024


def _dt_kernel(h_ref, wt_ref, bias_ref, o_ref):
    raw = lax.dot_general(wt_ref[...], h_ref[0], (((1,), (1,)), ((), ())), preferred_element_type=F32)
    v = raw + bias_ref[...]
    o_ref[0] = jnp.maximum(v, 0.0) + jnp.log1p(jnp.exp(-jnp.abs(v)))


def _dt_proj(h, w_dt_t, dt_bias):
    blocks = [((DT_TS, D_MODEL), BF16), ((SSM_HEADS, D_MODEL), BF16), ((SSM_HEADS, DT_TS), F32)]
    return pl.pallas_call(
        _dt_kernel,
        grid=(BATCH, SEQ // DT_TS),
        in_specs=[pl.BlockSpec((1, DT_TS, D_MODEL), lambda b, i: (b, i, 0)),
                  pl.BlockSpec((SSM_HEADS, D_MODEL), lambda b, i: (0, 0)),
                  pl.BlockSpec((SSM_HEADS, 1), lambda b, i: (0, 0))],
        out_specs=pl.BlockSpec((1, SSM_HEADS, DT_TS), lambda b, i: (b, 0, i)),
        out_shape=jax.ShapeDtypeStruct((BATCH, SSM_HEADS, SEQ), F32),
        compiler_params=_params(("arbitrary", "arbitrary"), blocks),
        name="dt_proj",
    )(h, w_dt_t, dt_bias.reshape(SSM_HEADS, 1))


ATTN_SPAN = 2048
SUB_BLOCKS = ATTN_SPAN // ATTN_BLK
MERGE_ROWS = 256


def _bucket_maps():
    qi = np.arange(ATTN_BLK)[:, None]
    kj = np.arange(2 * ATTN_BLK)[None, :]
    steps = np.clip(ATTN_BLK + qi - kj, 0, ATTN_BLK)
    exact = REL_BUCKETS // 2
    maps = []
    for _, dil in DILATED_GROUPS:
        dist = steps * dil
        n = np.maximum(dist, 1).astype(np.float32)
        large = exact + (np.log(n / np.float32(exact)) / np.float32(math.log(REL_MAX_DIST / exact))
                         * np.float32(REL_BUCKETS - exact)).astype(np.int32)
        large = np.minimum(large, REL_BUCKETS - 1)
        maps.append(np.where(dist < exact, dist, large).astype(np.int32))
    return np.stack(maps)


def _attn_kernel(tab_ref, bucket_ref, *refs):
    qkv_refs = refs[:5 * N_ATTN_GROUPS]
    out_ref, o_scr, l_scr, bias_ref, s_scr, p_scr, d_scr = refs[5 * N_ATTN_GROUPS:]
    b, hd, n = pl.program_id(0), pl.program_id(1), pl.program_id(2)

    kj2 = lax.broadcasted_iota(jnp.int32, (ATTN_BLK, 2 * ATTN_BLK), 1)
    qi2 = lax.broadcasted_iota(jnp.int32, (ATTN_BLK, 2 * ATTN_BLK), 0)

    @pl.when((b == 0) & (n == 0))
    def _():
        band = (kj2 >= qi2) & (kj2 <= qi2 + ATTN_BLK)
        for g in range(N_ATTN_GROUPS):
            bucket = bucket_ref[g]
            bias = jnp.zeros((ATTN_BLK, 2 * ATTN_BLK), F32)
            for k in range(REL_BUCKETS):
                bias = jnp.where(bucket == k, tab_ref[k, g * HEADS_PER_GROUP + hd], bias)
            bias_ref[g * HEADS_PER_GROUP + hd] = jnp.where(band, bias * LOG2_E, -jnp.inf)

    has_prev = (kj2 >= ATTN_BLK) | (n > 0)
    scale = LOG2_E / math.sqrt(HEAD_DIM)
    nt = (((1,), (1,)), ((), ()))

    for g, (_, dil) in enumerate(DILATED_GROUPS):
        q_ref, kc_ref, kp_ref, vc_ref, vp_ref = qkv_refs[5 * g:5 * g + 5]
        bias = bias_ref[g * HEADS_PER_GROUP + hd]
        bias_first = jnp.where(has_prev, bias, -jnp.inf)
        per_residue = SUB_BLOCKS // dil

        def rows(r, m, dil=dil):
            start = r + m * ATTN_BLK * dil
            return pl.ds(start, ATTN_BLK, stride=dil) if dil > 1 else pl.ds(start, ATTN_BLK)

        def both(prev_ref, cur_ref, r, m):
            prev = prev_ref[0, 0, rows(r, per_residue - 1), :] if m == 0 else cur_ref[0, 0, rows(r, m - 1), :]
            return jnp.concatenate([prev.astype(BF16), cur_ref[0, 0, rows(r, m), :].astype(BF16)], axis=0)

        subs = [(r, m) for r in range(dil) for m in range(per_residue)]
        for i, (r, m) in enumerate(subs):
            q = q_ref[0, 0, rows(r, m), :].astype(BF16)
            s = lax.dot_general(q, both(kp_ref, kc_ref, r, m), nt, preferred_element_type=F32)
            s_scr[i] = s * scale + (bias_first if m == 0 else bias)
        for i, (r, m) in enumerate(subs):
            s = s_scr[i]
            mx = jnp.max(s, axis=-1, keepdims=True)
            p = jnp.exp2(s - mx)
            den = jnp.sum(p, axis=-1, keepdims=True)
            p_scr[i] = p.astype(BF16)
            d_scr[i] = jnp.broadcast_to(1.0 / den, (ATTN_BLK, HEAD_DIM))
            l_scr[g, rows(r, m), :] = jnp.broadcast_to(mx + jnp.log2(den), (ATTN_BLK, HEAD_DIM))
        for i, (r, m) in enumerate(subs):
            acc = jnp.dot(p_scr[i], both(vp_ref, vc_ref, r, m), preferred_element_type=F32)
            o_scr[g, rows(r, m), :] = acc * d_scr[i]

    for c in range(ATTN_SPAN // MERGE_ROWS):
        sl = pl.ds(c * MERGE_ROWS, MERGE_ROWS)
        l0, l1, l2 = l_scr[0, sl, :], l_scr[1, sl, :], l_scr[2, sl, :]
        mx = jnp.maximum(jnp.maximum(l0, l1), l2)
        e0, e1, e2 = jnp.exp2(l0 - mx), jnp.exp2(l1 - mx), jnp.exp2(l2 - mx)
        attn = (e0 * o_scr[0, sl, :] + e1 * o_scr[1, sl, :] + e2 * o_scr[2, sl, :]) / (e0 + e1 + e2)
        out_ref[0, sl, :] = attn.astype(out_ref.dtype)


def _attention(qkv, rel_bias):
    blk = (1, 1, ATTN_SPAN, HEAD_DIM)

    def spec(which, g, prev):
        base = which * N_ATTN_HEADS + g * HEADS_PER_GROUP
        if prev:
            return pl.BlockSpec(blk, lambda b, hd, n: (b, base + hd, jnp.maximum(n - 1, 0), 0))
        return pl.BlockSpec(blk, lambda b, hd, n: (b, base + hd, n, 0))

    qkv_specs = []
    for g in range(N_ATTN_GROUPS):
        qkv_specs += [spec(0, g, False), spec(1, g, False), spec(1, g, True), spec(2, g, False), spec(2, g, True)]
    bias_shape = (N_ATTN_HEADS, ATTN_BLK, 2 * ATTN_BLK)
    scr_shape = (N_ATTN_GROUPS, ATTN_SPAN, HEAD_DIM)
    stage_shape = (SUB_BLOCKS, ATTN_BLK, 2 * ATTN_BLK)
    blocks =[(blk, F32)] * len(qkv_specs) + [((ATTN_SPAN, HEAD_DIM), BF16), (bias_shape, jnp.int32)]
    return pl.pallas_call(
        _attn_kernel,
        grid=(BATCH, HEADS_PER_GROUP, SEQ // ATTN_SPAN),
        in_specs=[pl.BlockSpec(memory_space=pltpu.SMEM),
                  pl.BlockSpec((N_ATTN_GROUPS, ATTN_BLK, 2 * ATTN_BLK), lambda b, hd, n: (0, 0, 0))] + qkv_specs,
        out_specs=pl.BlockSpec((1, ATTN_SPAN, HEAD_DIM), lambda b, hd, n: (b, n, hd)),
        out_shape=jax.ShapeDtypeStruct((BATCH, SEQ, ATTN_OUT), BF16),
        scratch_shapes=[pltpu.VMEM(scr_shape, F32), pltpu.VMEM(scr_shape, F32), pltpu.VMEM(bias_shape, F32),
                        pltpu.VMEM(stage_shape, F32), pltpu.VMEM(stage_shape, BF16),
                        pltpu.VMEM((SUB_BLOCKS, ATTN_BLK, HEAD_DIM), F32)],
        compiler_params=_params(("arbitrary", "arbitrary", "arbitrary"), blocks,
                                extra=2 * _nbytes(scr_shape, F32) + _nbytes(bias_shape, F32)
                                + _nbytes(stage_shape, F32) + _nbytes(stage_shape, BF16)
                                + _nbytes((SUB_BLOCKS, ATTN_BLK, HEAD_DIM), F32)),
        name="dilated_attn",
    )(rel_bias, jnp.asarray(_bucket_maps()), *([qkv] * len(qkv_specs)))


N_CHUNKS = SEQ // CHUNK
HEAD_PAIRS = HEADS_PER_SSM_GROUP // 2


def _split3(v):
    hi = v.astype(BF16)
    r1 = v - hi.astype(F32)
    mid = r1.astype(BF16)
    lo = (r1 - mid.astype(F32)).astype(BF16)
    return hi, mid, lo


def _ssd_kernel(xs_ref, bm_ref, cm_ref, z_ref, dtt_ref, cw_ref, cb_ref, acol_ref, dskip_ref, nw_ref,
                y_ref, state_ref):
    state_ref[...] = jnp.zeros_like(state_ref)

    li = lax.broadcasted_iota(jnp.int32, (CHUNK, CHUNK), 0)
    si = lax.broadcasted_iota(jnp.int32, (CHUNK, CHUNK), 1)
    tril = li >= si
    ltri = jnp.where(tril, 1.0, 0.0).astype(BF16)
    utri = jnp.where(li <= si, 1.0, 0.0).astype(BF16)
    low_half = si < SSM_HEAD_DIM
    low_row = lax.broadcasted_iota(jnp.int32, (1, LANES), 1) < SSM_HEAD_DIM
    nt = (((1,), (1,)), ((), ()))
    ri = lax.broadcasted_iota(jnp.int32, (CONV_K * CHUNK, 2 * CHUNK), 0)
    ci = lax.broadcasted_iota(jnp.int32, (CONV_K * CHUNK, 2 * CHUNK), 1)
    shift = jnp.where(ci == CHUNK + (ri & (CHUNK - 1)) - (ri >> 7), 1.0, 0.0).astype(BF16)

    cw = cw_ref[0]
    cb = cb_ref[0]
    a_col = -jnp.exp(acol_ref[0]) * LOG2_E
    dskip = dskip_ref[0]
    nw = nw_ref[...]

    def load(rows):
        return jnp.concatenate([xs_ref[0, rows, :], bm_ref[0, rows, :], cm_ref[0, rows, :]], axis=1)

    def chunk(c, carry):
        r0 = pl.multiple_of(c * CHUNK, CHUNK)
        rows = pl.ds(r0, CHUNK)
        u_prev = load(pl.ds(pl.multiple_of(jnp.maximum(r0 - CHUNK, 0), CHUNK), CHUNK))
        u_prev = jnp.where(c > 0, u_prev, jnp.zeros_like(u_prev))
        taps = jnp.dot(shift, jnp.concatenate([u_prev, load(rows)], axis=0), preferred_element_type=F32)
        acc = cb + cw[CONV_K - 1:CONV_K, :] * taps[0:CHUNK]
        for s in range(1, CONV_K):
            acc = acc + cw[CONV_K - 1 - s:CONV_K - s, :] * taps[s * CHUNK:(s + 1) * CHUNK]
        xbc = _silu(acc)
        x = xbc[:, :GROUP_W]
        bmat = xbc[:, GROUP_W:GROUP_W + D_STATE]
        cmat = xbc[:, GROUP_W + D_STATE:]

        dtt = dtt_ref[0, :, rows]
        dat3 = _split3(dtt * a_col)
        acum_t = sum(jnp.dot(p, utri, preferred_element_type=F32) for p in dat3)
        acum = sum(lax.dot_general(ltri, p, nt, preferred_element_type=F32) for p in dat3)
        w_t = dtt * jnp.exp2(acum_t[:, CHUNK - 1:CHUNK] - acum_t)
        cdec = jnp.exp2(acum[CHUNK - 1:CHUNK, :])

        bt = bmat.T
        cbm = jnp.dot(cmat.astype(BF16), bt.astype(BF16), preferred_element_type=F32)
        x_b = x.astype(BF16)

        y_blocks = []
        for k in range(HEAD_PAIRS):
            cols = slice(k * LANES, (k + 1) * LANES)
            xk = x_b[:, cols]
            zero = jnp.zeros_like(xk)
            x_lo, x_hi = jnp.where(low_half, xk, zero), jnp.where(low_half, zero, xk)
            st = state_ref[:, cols]
            st_b = st.astype(BF16)
            st_lo, st_hi = jnp.where(low_half, st_b, zero), jnp.where(low_half, zero, st_b)
            ms, cs, bts, cds = [], [], [], []
            for r in (2 * k, 2 * k + 1):
                col = jnp.broadcast_to(acum[:, r:r + 1], (CHUNK, CHUNK))
                row = jnp.broadcast_to(acum_t[r:r + 1, :], (CHUNK, CHUNK))
                ldec = jnp.exp2(jnp.where(tril, col - row, -jnp.inf))
                ms.append((cbm * ldec * jnp.broadcast_to(dtt[r:r + 1, :], (CHUNK, CHUNK))).astype(BF16))
                cs.append((cmat * jnp.exp2(col)).astype(BF16))
                bts.append((bt * jnp.broadcast_to(w_t[r:r + 1, :], (CHUNK, CHUNK))).astype(BF16))
                cds.append(jnp.broadcast_to(cdec[:, r:r + 1], (1, LANES)))
            y_blocks.append(jnp.dot(jnp.concatenate(ms + cs, axis=1),
                                    jnp.concatenate([x_lo, x_hi, st_lo, st_hi], axis=0),
                                    preferred_element_type=F32))
            upd = jnp.dot(jnp.concatenate(bts, axis=1), jnp.concatenate([x_lo, x_hi], axis=0),
                          preferred_element_type=F32)
            state_ref[:, cols] = st * jnp.where(low_row, cds[0], cds[1]) + upd
        y = jnp.concatenate(y_blocks, axis=1)

        y = y + dskip * x
        y = y * _silu(z_ref[0, rows, :].astype(F32))
        y = y * lax.rsqrt(jnp.mean(y * y, axis=-1, keepdims=True) + EPS)
        y_ref[0, rows, :] = (y * nw).astype(y_ref.dtype)
        return carry

    lax.fori_loop(0, N_CHUNKS, chunk, 0, unroll=4)


def _ssd(proj, dtt, conv_w_g, conv_b_g, a_col, dskip_x, norm_w):
    view = proj.reshape(BATCH, SEQ, PROJ_W)
    gw, ns = GROUP_W, D_STATE
    blocks = [((SEQ, gw), BF16)] * 3 + [((SEQ, ns), BF16)] * 2 + [((HEADS_PER_SSM_GROUP, SEQ), F32)]
    return pl.pallas_call(
        _ssd_kernel,
        grid=(BATCH, SSM_GROUPS),
        in_specs=[pl.BlockSpec((1, SEQ, gw), lambda b, g: (b, 0, P_XBC // gw + g)),
                  pl.BlockSpec((1, SEQ, ns), lambda b, g: (b, 0, P_B // ns + g)),
                  pl.BlockSpec((1, SEQ, ns), lambda b, g: (b, 0, P_C // ns + g)),
                  pl.BlockSpec((1, SEQ, gw), lambda b, g: (b, 0, P_Z // gw + g)),
                  pl.BlockSpec((1, HEADS_PER_SSM_GROUP, SEQ), lambda b, g: (b, g, 0)),
                  pl.BlockSpec((1, CONV_K, CONV_GROUP_W), lambda b, g: (g, 0, 0)),
                  pl.BlockSpec((1, 1, CONV_GROUP_W), lambda b, g: (g, 0, 0)),
                  pl.BlockSpec((1, HEADS_PER_SSM_GROUP, LANES), lambda b, g: (g, 0, 0)),
                  pl.BlockSpec((1, 1, gw), lambda b, g: (g, 0, 0)),
                  pl.BlockSpec((1, gw), lambda b, g: (0, g))],
        out_specs=pl.BlockSpec((1, SEQ, gw), lambda b, g: (b, 0, g)),
        out_shape=jax.ShapeDtypeStruct((BATCH, SEQ, D_INNER), BF16),
        scratch_shapes=[pltpu.VMEM((ns, gw), F32)],
        compiler_params=_params(("arbitrary", "arbitrary"), blocks, extra=_nbytes((ns, gw), F32) + (8 << 20)),
        name="ssd_scan",
    )(view, view, view, view, dtt, conv_w_g, conv_b_g, a_col, dskip_x, norm_w.reshape(1, D_INNER))


MG_TM = 1024
MG_TN = 512


def _merge_kernel(attn_ref, y_ref, ga_ref, gs_ref, wa_ref, ws_ref, out_ref):
    pa = jnp.dot(attn_ref[...], wa_ref[0], preferred_element_type=F32)
    ps = jnp.dot(y_ref[...], ws_ref[0], preferred_element_type=F32)
    merged = _sigmoid(ga_ref[...].astype(F32)) * pa + _sigmoid(gs_ref[...].astype(F32)) * ps
    out_ref[...] = merged.astype(out_ref.dtype)


def _merge_proj(attn, y, gates, w_attn, w_ssm, layer):
    row = lambda i, j: (i, 0)
    blocks = [((MG_TM, ATTN_OUT), BF16), ((MG_TM, D_INNER), BF16)] + [((MG_TM, MG_TN), BF16)] * 3 \
        + [((ATTN_OUT, MG_TN), BF16), ((D_INNER, MG_TN), BF16)]
    return pl.pallas_call(
        _merge_kernel,
        grid=(TOKENS // MG_TM, D_MODEL // MG_TN),
        in_specs=[pl.BlockSpec((MG_TM, ATTN_OUT), row),
                  pl.BlockSpec((MG_TM, D_INNER), row),
                  pl.BlockSpec((MG_TM, MG_TN), lambda i, j: (i, j)),
                  pl.BlockSpec((MG_TM, MG_TN), lambda i, j: (i, D_MODEL // MG_TN + j)),
                  pl.BlockSpec((1, ATTN_OUT, MG_TN), lambda i, j: (layer, 0, j)),
                  pl.BlockSpec((1, D_INNER, MG_TN), lambda i, j: (layer, 0, j))],
        out_specs=pl.BlockSpec((MG_TM, MG_TN), lambda i, j: (i, j)),
        out_shape=jax.ShapeDtypeStruct((TOKENS, D_MODEL), BF16),
        compiler_params=_params(("arbitrary", "arbitrary"), blocks, extra=4 * _nbytes((MG_TM, MG_TN), F32)),
        name="merge_proj",
    )(attn, y, gates, gates, w_attn, w_ssm)


def _resid_kernel(a_ref, w_ref, x_ref, g_ref, o_ref):
    o_ref[...] = x_ref[...] + g_ref[0] * jnp.dot(a_ref[...], w_ref[0], preferred_element_type=F32)


def _resid_proj(a, w, layer, x, gate, name):
    k = a.shape[1]
    tm, tn = 1024, 512
    blocks = [((tm, k), BF16), ((k, tn), BF16), ((tm, tn), F32), ((tm, tn), F32)]
    return pl.pallas_call(
        _resid_kernel,
        grid=(TOKENS // tm, D_MODEL // tn),
        in_specs=[pl.BlockSpec((tm, k), lambda i, j: (i, 0)),
                  pl.BlockSpec((1, k, tn), lambda i, j: (layer, 0, j)),
                  pl.BlockSpec((tm, tn), lambda i, j: (i, j)),
                  pl.BlockSpec((1, 1, tn), lambda i, j: (i * tm // SEQ, 0, j))],
        out_specs=pl.BlockSpec((tm, tn), lambda i, j: (i, j)),
        out_shape=jax.ShapeDtypeStruct((TOKENS, D_MODEL), F32),
        compiler_params=_params(("arbitrary", "arbitrary"), blocks, extra=_nbytes((tm, tn), F32)),
        name=name,
    )(a, w, x, gate)


RN_TM = 512


def _resid_norm_kernel(a_ref, w_ref, x_ref, g_ref, nw_ref, sc_ref, sh_ref, o_ref, h_ref):
    x1 = x_ref[...] + g_ref[0] * jnp.dot(a_ref[...], w_ref[0], preferred_element_type=F32)
    o_ref[...] = x1
    y = x1 * lax.rsqrt(jnp.mean(x1 * x1, axis=-1, keepdims=True) + EPS)
    y = y * nw_ref[...]
    h_ref[...] = (y * (1.0 + sc_ref[0]) + sh_ref[0]).astype(h_ref.dtype)


def _resid_norm_proj(a, w, layer, x, gate, norm_w, sc, sh):
    tm = RN_TM
    row = lambda i: (i, 0)
    per_batch = lambda i: (i * tm // SEQ, 0, 0)
    blocks = [((tm, D_MODEL), BF16), ((D_MODEL, D_MODEL), BF16), ((tm, D_MODEL), F32), ((tm, D_MODEL), F32),
              ((tm, D_MODEL), BF16)]
    return pl.pallas_call(
        _resid_norm_kernel,
        grid=(TOKENS // tm,),
        in_specs=[pl.BlockSpec((tm, D_MODEL), row),
                  pl.BlockSpec((1, D_MODEL, D_MODEL), lambda i: (layer, 0, 0)),
                  pl.BlockSpec((tm, D_MODEL), row),
                  pl.BlockSpec((1, 1, D_MODEL), per_batch),
                  pl.BlockSpec((1, D_MODEL), lambda i: (0, 0)),
                  pl.BlockSpec((1, 1, D_MODEL), per_batch),
                  pl.BlockSpec((1, 1, D_MODEL), per_batch)],
        out_specs=[pl.BlockSpec((tm, D_MODEL), row), pl.BlockSpec((tm, D_MODEL), row)],
        out_shape=[jax.ShapeDtypeStruct((TOKENS, D_MODEL), F32), jax.ShapeDtypeStruct((TOKENS, D_MODEL), BF16)],
        compiler_params=_params(("arbitrary",), blocks, extra=2 * _nbytes((tm, D_MODEL), F32)),
        name="out_proj_resid_norm",
    )(a, w, x, gate, norm_w.reshape(1, D_MODEL), sc, sh)


FF_TM = 1024
FF_TN = 512


def _swiglu_kernel(h_ref, wg_ref, wu_ref, o_ref):
    h = h_ref[...]
    hg = jnp.dot(h, wg_ref[0], preferred_element_type=F32)
    hu = jnp.dot(h, wu_ref[0], preferred_element_type=F32)
    o_ref[...] = (_silu(hg) * hu).astype(o_ref.dtype)


def _swiglu_in(h, w, layer):
    nblk = D_FF // FF_TN
    blocks = [((FF_TM, D_MODEL), BF16), ((D_MODEL, FF_TN), BF16), ((D_MODEL, FF_TN), BF16), ((FF_TM, FF_TN), BF16)]
    return pl.pallas_call(
        _swiglu_kernel,
        grid=(TOKENS // FF_TM, nblk),
        in_specs=[pl.BlockSpec((FF_TM, D_MODEL), lambda i, j: (i, 0)),
                  pl.BlockSpec((1, D_MODEL, FF_TN), lambda i, j: (layer, 0, j)),
                  pl.BlockSpec((1, D_MODEL, FF_TN), lambda i, j: (layer, 0, nblk + j))],
        out_specs=pl.BlockSpec((FF_TM, FF_TN), lambda i, j: (i, j)),
        out_shape=jax.ShapeDtypeStruct((TOKENS, D_FF), BF16),
        compiler_params=_params(("arbitrary", "arbitrary"), blocks, extra=3 * _nbytes((FF_TM, FF_TN), F32)),
        name="swiglu_in",
    )(h, w, w)


def _mixer(x, h, l, rel_bias, w_in, w_in_b, conv_w, conv_b, dt_bias, a_log, d_skip, ssm_norm_w,
           w_attn_b, w_ssm_b, w_out_b, gate, norm2):
    h2d = h.reshape(TOKENS, D_MODEL)
    qkv = _qkv_proj(h, w_in_b, l)
    proj = _matmul(h2d, w_in_b, l, OFF_Z, PROJ_W, 2048, 512, "in_proj")
    w_gates = w_in_b[l:l + 1, :, OFF_GA:]
    gates = _matmul(h2d, w_gates, 0, 0, 2 * D_MODEL, 2048, 512, "gate_proj")
    w_dt_t = w_in[l, :, OFF_DT:OFF_GA].T.astype(BF16)
    dtt = _dt_proj(h, w_dt_t, dt_bias[l])

    attn = _attention(qkv, rel_bias)

    cw, cbias = conv_w[l], conv_b[l]
    xs_w = cw[:, :D_INNER].reshape(CONV_K, SSM_GROUPS, GROUP_W)
    b_w = cw[:, D_INNER:D_INNER + SSM_GROUPS * D_STATE].reshape(CONV_K, SSM_GROUPS, D_STATE)
    c_w = cw[:, D_INNER + SSM_GROUPS * D_STATE:].reshape(CONV_K, SSM_GROUPS, D_STATE)
    conv_w_g = jnp.concatenate([xs_w, b_w, c_w], axis=2).transpose(1, 0, 2)
    conv_b_g = jnp.concatenate([cbias[:D_INNER].reshape(SSM_GROUPS, GROUP_W),
                                cbias[D_INNER:D_INNER + SSM_GROUPS * D_STATE].reshape(SSM_GROUPS, D_STATE),
                                cbias[D_INNER + SSM_GROUPS * D_STATE:].reshape(SSM_GROUPS, D_STATE)],
                               axis=1).reshape(SSM_GROUPS, 1, CONV_GROUP_W)
    a_col = jnp.broadcast_to(a_log[l].reshape(SSM_GROUPS, HEADS_PER_SSM_GROUP, 1),
                             (SSM_GROUPS, HEADS_PER_SSM_GROUP, LANES))
    dskip_x = jnp.repeat(d_skip[l], SSM_HEAD_DIM).reshape(SSM_GROUPS, 1, GROUP_W)
    y = _ssd(proj, dtt, conv_w_g, conv_b_g, a_col, dskip_x, ssm_norm_w[l])

    merged = _merge_proj(attn.reshape(TOKENS, ATTN_OUT), y.reshape(TOKENS, D_INNER), gates, w_attn_b, w_ssm_b, l)
    return _resid_norm_proj(merged, w_out_b, l, x, gate, *norm2)


def kernel(x, c, rel_bias, norm1_w, norm2_w, w_mod, b_mod, w_in, conv_w, conv_b, dt_bias, a_log, d_skip,
           ssm_norm_w, w_attn_proj, w_ssm_proj, w_out, w_ffn_in, w_ffn_out, final_norm_w):
    mod = _modulation(c, w_mod, b_mod)[:, :BATCH]
    xt = x.reshape(TOKENS, D_MODEL)
    w_in_b, w_attn_b, w_ssm_b, w_out_b, w_ffn_in_b, w_ffn_out_b = [
        _to_bf16(w) for w in (w_in, w_attn_proj, w_ssm_proj, w_out, w_ffn_in, w_ffn_out)]
    for l in range(DEPTH):
        sh1, sc1, g1, sh2, sc2, g2 = [m.reshape(BATCH, 1, D_MODEL) for m in jnp.split(mod[l], 6, axis=-1)]
        h = _norm(xt.reshape(BATCH, SEQ, D_MODEL), norm1_w[l], sc1, sh1, BF16, True)
        xt, h = _mixer(xt, h, l, rel_bias, w_in, w_in_b, conv_w, conv_b, dt_bias, a_log, d_skip, ssm_norm_w,
                       w_attn_b, w_ssm_b, w_out_b, g1, (norm2_w[l], sc2, sh2))
        u = _swiglu_in(h, w_ffn_in_b, l)
        xt = _resid_proj(u, w_ffn_out_b, l, xt, g2, "ffn_out_resid")
    zeros = jnp.zeros((BATCH, 1, D_MODEL), F32)
    return _norm(xt.reshape(BATCH, SEQ, D_MODEL), final_norm_w, zeros, zeros, F32, False)
```

```python
import functools
import math

import jax
import jax.numpy as jnp
import numpy as np
from jax import lax
from jax.experimental import pallas as pl
from jax.experimental.pallas import tpu as pltpu

F32 = jnp.float32
BF16 = jnp.bfloat16

D_MODEL = 2048
BATCH = 4
SEQ = 4096
TOKENS = BATCH * SEQ
DEPTH = 2
DILATED_GROUPS = ((128, 1), (512, 4), (2048, 16))
N_ATTN_GROUPS = len(DILATED_GROUPS)
HEADS_PER_GROUP = 4
N_ATTN_HEADS = N_ATTN_GROUPS * HEADS_PER_GROUP
HEAD_DIM = 128
QKV_W = N_ATTN_HEADS * HEAD_DIM
ATTN_OUT = HEADS_PER_GROUP * HEAD_DIM
ATTN_BLK = 128
REL_BUCKETS = 32
REL_MAX_DIST = 2048
D_INNER = 2 * D_MODEL
SSM_HEAD_DIM = 64
SSM_HEADS = D_INNER // SSM_HEAD_DIM
SSM_GROUPS = 8
HEADS_PER_SSM_GROUP = SSM_HEADS // SSM_GROUPS
GROUP_W = HEADS_PER_SSM_GROUP * SSM_HEAD_DIM
D_STATE = 128
CONV_K = 4
CHUNK = 128
CONV_DIM = D_INNER + 2 * SSM_GROUPS * D_STATE
CONV_GROUP_W = GROUP_W + 2 * D_STATE
D_FF = 5632
SPLITS = (QKV_W, QKV_W, QKV_W, D_INNER, CONV_DIM, SSM_HEADS, D_MODEL, D_MODEL)
OFF_Q, OFF_K, OFF_V, OFF_Z, OFF_XBC, OFF_DT, OFF_GA, OFF_GS = np.cumsum((0,) + SPLITS[:-1]).tolist()
N_QKV_HEADS = 3 * N_ATTN_HEADS
PROJ_W = D_INNER + CONV_DIM
P_Z = 0
P_XBC = D_INNER
P_B = P_XBC + D_INNER
P_C = P_B + SSM_GROUPS * D_STATE
EPS = 1e-6
LOG2_E = math.log2(math.e)

LANES = 128
VMEM_LIMIT_CAP = 60 * 1024 * 1024


def _nbytes(shape, dtype):
    return int(np.prod(shape)) * jnp.dtype(dtype).itemsize


def _params(sem, blocks, extra=0):
    need = 2 * sum(_nbytes(s, d) for s, d in blocks) + extra
    return pltpu.CompilerParams(dimension_semantics=sem,
                                vmem_limit_bytes=min(VMEM_LIMIT_CAP, need + (8 << 20)))


def _sigmoid(v):
    return 0.5 + 0.5 * jnp.tanh(0.5 * v)


def _silu(v):
    h = 0.5 * v
    return h + h * jnp.tanh(h)


MOD_TN = 1024
MOD_ROWS = 8


def _mod_kernel(c_ref, w_ref, b_ref, o_ref):
    ca = _silu(c_ref[...])
    o_ref[0] = jnp.dot(ca, w_ref[0], preferred_element_type=F32) + b_ref[0]


def _modulation(c, w_mod, b_mod):
    c8 = jnp.zeros((MOD_ROWS, D_MODEL), F32).at[:BATCH].set(c)
    n = 6 * D_MODEL
    blocks = [((MOD_ROWS, D_MODEL), F32), ((D_MODEL, MOD_TN), F32), ((1, MOD_TN), F32), ((MOD_ROWS, MOD_TN), F32)]
    return pl.pallas_call(
        _mod_kernel,
        grid=(DEPTH, n // MOD_TN),
        in_specs=[pl.BlockSpec((MOD_ROWS, D_MODEL), lambda l, j: (0, 0)),
                  pl.BlockSpec((1, D_MODEL, MOD_TN), lambda l, j: (l, 0, j)),
                  pl.BlockSpec((1, 1, MOD_TN), lambda l, j: (l, 0, j))],
        out_specs=pl.BlockSpec((1, MOD_ROWS, MOD_TN), lambda l, j: (l, 0, j)),
        out_shape=jax.ShapeDtypeStruct((DEPTH, MOD_ROWS, n), F32),
        compiler_params=_params(("arbitrary", "arbitrary"), blocks),
        name="modulation",
    )(c8, w_mod, b_mod.reshape(DEPTH, 1, n))


NORM_TS = 1024


def _norm_kernel(x_ref, w_ref, sc_ref, sh_ref, o_ref, *, modulate):
    x = x_ref[0]
    y = x * lax.rsqrt(jnp.mean(x * x, axis=-1, keepdims=True) + EPS)
    y = y * w_ref[...]
    if modulate:
        y = y * (1.0 + sc_ref[0]) + sh_ref[0]
    o_ref[0] = y.astype(o_ref.dtype)


def _norm(x, w, sc, sh, out_dtype, modulate):
    blocks = [((NORM_TS, D_MODEL), F32), ((NORM_TS, D_MODEL), out_dtype)]
    return pl.pallas_call(
        functools.partial(_norm_kernel, modulate=modulate),
        grid=(BATCH, SEQ // NORM_TS),
        in_specs=[pl.BlockSpec((1, NORM_TS, D_MODEL), lambda b, i: (b, i, 0)),
                  pl.BlockSpec((1, D_MODEL), lambda b, i: (0, 0)),
                  pl.BlockSpec((1, 1, D_MODEL), lambda b, i: (b, 0, 0)),
                  pl.BlockSpec((1, 1, D_MODEL), lambda b, i: (b, 0, 0))],
        out_specs=pl.BlockSpec((1, NORM_TS, D_MODEL), lambda b, i: (b, i, 0)),
        out_shape=jax.ShapeDtypeStruct((BATCH, SEQ, D_MODEL), out_dtype),
        compiler_params=_params(("arbitrary", "arbitrary"), blocks, extra=3 * _nbytes((NORM_TS, D_MODEL), F32)),
        name="rmsnorm_mod" if modulate else "rmsnorm",
    )(x, w.reshape(1, D_MODEL), sc, sh)


def _mm_kernel(a_ref, w_ref, o_ref):
    o_ref[...] = jnp.dot(a_ref[...], w_ref[0], preferred_element_type=F32).astype(o_ref.dtype)


def _matmul(a, w, layer, col0, n, tm, tn, name):
    m, k = a.shape
    blocks = [((tm, k), BF16), ((k, tn), BF16), ((tm, tn), BF16)]
    return pl.pallas_call(
        _mm_kernel,
        grid=(m // tm, n // tn),
        in_specs=[pl.BlockSpec((tm, k), lambda i, j: (i, 0)),
                  pl.BlockSpec((1, k, tn), lambda i, j: (layer, 0, col0 // tn + j))],
        out_specs=pl.BlockSpec((tm, tn), lambda i, j: (i, j)),
        out_shape=jax.ShapeDtypeStruct((m, n), BF16),
        compiler_params=_params(("arbitrary", "arbitrary"), blocks, extra=_nbytes((tm, tn), F32)),
        name=name,
    )(a, w)


QKV_TM = 2048
QKV_HEADS_PER_STEP = 6


def _qkv_kernel(h_ref, w_ref, o_ref):
    res = jnp.dot(h_ref[0], w_ref[0], preferred_element_type=F32)
    for hd in range(QKV_HEADS_PER_STEP):
        o_ref[0, hd] = res[:, hd * HEAD_DIM:(hd + 1) * HEAD_DIM]


def _qkv_proj(h, w, layer):
    tn = QKV_HEADS_PER_STEP * HEAD_DIM
    blocks = [((QKV_TM, D_MODEL), BF16), ((D_MODEL, tn), BF16), ((QKV_TM, tn), F32)]
    return pl.pallas_call(
        _qkv_kernel,
        grid=(BATCH, SEQ // QKV_TM, N_QKV_HEADS // QKV_HEADS_PER_STEP),
        in_specs=[pl.BlockSpec((1, QKV_TM, D_MODEL), lambda b, i, j: (b, i, 0)),
                  pl.BlockSpec((1, D_MODEL, tn), lambda b, i, j: (layer, 0, j))],
        out_specs=pl.BlockSpec((1, QKV_HEADS_PER_STEP, QKV_TM, HEAD_DIM), lambda b, i, j: (b, j, i, 0)),
        out_shape=jax.ShapeDtypeStruct((BATCH, N_QKV_HEADS, SEQ, HEAD_DIM), F32),
        compiler_params=_params(("arbitrary", "arbitrary", "arbitrary"), blocks, extra=_nbytes((QKV_TM, tn), F32)),
        name="qkv_proj",
    )(h, w)


DT_TS = 1024


def _dt_kernel(h_ref, wt_ref, bias_ref, o_ref):
    raw = lax.dot_general(wt_ref[...], h_ref[0], (((1,), (1,)), ((), ())), preferred_element_type=F32)
    v = raw + bias_ref[...]
    o_ref[0] = jnp.maximum(v, 0.0) + jnp.log1p(jnp.exp(-jnp.abs(v)))


def _dt_proj(h, w_dt_t, dt_bias):
    blocks = [((DT_TS, D_MODEL), BF16), ((SSM_HEADS, D_MODEL), BF16), ((SSM_HEADS, DT_TS), F32)]
    return pl.pallas_call(
        _dt_kernel,
        grid=(BATCH, SEQ // DT_TS),
        in_specs=[pl.BlockSpec((1, DT_TS, D_MODEL), lambda b, i: (b, i, 0)),
                  pl.BlockSpec((SSM_HEADS, D_MODEL), lambda b, i: (0, 0)),
                  pl.BlockSpec((SSM_HEADS, 1), lambda b, i: (0, 0))],
        out_specs=pl.BlockSpec((1, SSM_HEADS, DT_TS), lambda b, i: (b, 0, i)),
        out_shape=jax.ShapeDtypeStruct((BATCH, SSM_HEADS, SEQ), F32),
        compiler_params=_params(("arbitrary", "arbitrary"), blocks),
        name="dt_proj",
    )(h, w_dt_t, dt_bias.reshape(SSM_HEADS, 1))


ATTN_SPAN = 2048
SUB_BLOCKS = ATTN_SPAN // ATTN_BLK
MERGE_ROWS = 256


def _bucket_maps():
    qi = np.arange(ATTN_BLK)[:, None]
    kj = np.arange(2 * ATTN_BLK)[None, :]
    steps = np.clip(ATTN_BLK + qi - kj, 0, ATTN_BLK)
    exact = REL_BUCKETS // 2
    maps = []
    for _, dil in DILATED_GROUPS:
        dist = steps * dil
        n = np.maximum(dist, 1).astype(np.float32)
        large = exact + (np.log(n / np.float32(exact)) / np.float32(math.log(REL_MAX_DIST / exact))
                         * np.float32(REL_BUCKETS - exact)).astype(np.int32)
        large = np.minimum(large, REL_BUCKETS - 1)
        maps.append(np.where(dist < exact, dist, large).astype(np.int32))
    return np.stack(maps)


def _attn_kernel(tab_ref, bucket_ref, *refs):
    qkv_refs = refs[:5 * N_ATTN_GROUPS]
    out_ref, o_scr, l_scr, bias_ref, s_scr, p_scr, d_scr = refs[5 * N_ATTN_GROUPS:]
    b, hd, n = pl.program_id(0), pl.program_id(1), pl.program_id(2)

    kj2 = lax.broadcasted_iota(jnp.int32, (ATTN_BLK, 2 * ATTN_BLK), 1)
    qi2 = lax.broadcasted_iota(jnp.int32, (ATTN_BLK, 2 * ATTN_BLK), 0)

    @pl.when((b == 0) & (n == 0))
    def _():
        band = (kj2 >= qi2) & (kj2 <= qi2 + ATTN_BLK)
        for g in range(N_ATTN_GROUPS):
            bucket = bucket_ref[g]
            bias = jnp.zeros((ATTN_BLK, 2 * ATTN_BLK), F32)
            for k in range(REL_BUCKETS):
                bias = jnp.where(bucket == k, tab_ref[k, g * HEADS_PER_GROUP + hd], bias)
            bias_ref[g * HEADS_PER_GROUP + hd] = jnp.where(band, bias * LOG2_E, -jnp.inf)

    has_prev = (kj2 >= ATTN_BLK) | (n > 0)
    scale = LOG2_E / math.sqrt(HEAD_DIM)
    nt = (((1,), (1,)), ((), ()))

    for g, (_, dil) in enumerate(DILATED_GROUPS):
        q_ref, kc_ref, kp_ref, vc_ref, vp_ref = qkv_refs[5 * g:5 * g + 5]
        bias = bias_ref[g * HEADS_PER_GROUP + hd]
        bias_first = jnp.where(has_prev, bias, -jnp.inf)
        per_residue = SUB_BLOCKS // dil

        def rows(r, m, dil=dil):
            start = r + m * ATTN_BLK * dil
            return pl.ds(start, ATTN_BLK, stride=dil) if dil > 1 else pl.ds(start, ATTN_BLK)

        def both(prev_ref, cur_ref, r, m):
            prev = prev_ref[0, 0, rows(r, per_residue - 1), :] if m == 0 else cur_ref[0, 0, rows(r, m - 1), :]
            return jnp.concatenate([prev.astype(BF16), cur_ref[0, 0, rows(r, m), :].astype(BF16)], axis=0)

        subs = [(r, m) for r in range(dil) for m in range(per_residue)]
        for i, (r, m) in enumerate(subs):
            q = q_ref[0, 0, rows(r, m), :].astype(BF16)
            s = lax.dot_general(q, both(kp_ref, kc_ref, r, m), nt, preferred_element_type=F32)
            s_scr[i] = s * scale + (bias_first if m == 0 else bias)
        for i, (r, m) in enumerate(subs):
            s = s_scr[i]
            mx = jnp.max(s, axis=-1, keepdims=True)
            p = jnp.exp2(s - mx)
            den = jnp.sum(p, axis=-1, keepdims=True)
            p_scr[i] = p.astype(BF16)
            d_scr[i] = jnp.broadcast_to(1.0 / den, (ATTN_BLK, HEAD_DIM))
            l_scr[g, rows(r, m), :] = jnp.broadcast_to(mx + jnp.log2(den), (ATTN_BLK, HEAD_DIM))
        for i, (r, m) in enumerate(subs):
            acc = jnp.dot(p_scr[i], both(vp_ref, vc_ref, r, m), preferred_element_type=F32)
            o_scr[g, rows(r, m), :] = acc * d_scr[i]

    for c in range(ATTN_SPAN // MERGE_ROWS):
        sl = pl.ds(c * MERGE_ROWS, MERGE_ROWS)
        l0, l1, l2 = l_scr[0, sl, :], l_scr[1, sl, :], l_scr[2, sl, :]
        mx = jnp.maximum(jnp.maximum(l0, l1), l2)
        e0, e1, e2 = jnp.exp2(l0 - mx), jnp.exp2(l1 - mx), jnp.exp2(l2 - mx)
        attn = (e0 * o_scr[0, sl, :] + e1 * o_scr[1, sl, :] + e2 * o_scr[2, sl, :]) / (e0 + e1 + e2)
        out_ref[0, sl, :] = attn.astype(out_ref.dtype)


def _attention(qkv, rel_bias):
    blk = (1, 1, ATTN_SPAN, HEAD_DIM)

    def spec(which, g, prev):
        base = which * N_ATTN_HEADS + g * HEADS_PER_GROUP
        if prev:
            return pl.BlockSpec(blk, lambda b, hd, n: (b, base + hd, jnp.maximum(n - 1, 0), 0))
        return pl.BlockSpec(blk, lambda b, hd, n: (b, base + hd, n, 0))

    qkv_specs = []
    for g in range(N_ATTN_GROUPS):
        qkv_specs += [spec(0, g, False), spec(1, g, False), spec(1, g, True), spec(2, g, False), spec(2, g, True)]
    bias_shape = (N_ATTN_HEADS, ATTN_BLK, 2 * ATTN_BLK)
    scr_shape = (N_ATTN_GROUPS, ATTN_SPAN, HEAD_DIM)
    stage_shape = (SUB_BLOCKS, ATTN_BLK, 2 * ATTN_BLK)
    blocks =[(blk, F32)] * len(qkv_specs) + [((ATTN_SPAN, HEAD_DIM), BF16), (bias_shape, jnp.int32)]
    return pl.pallas_call(
        _attn_kernel,
        grid=(BATCH, HEADS_PER_GROUP, SEQ // ATTN_SPAN),
        in_specs=[pl.BlockSpec(memory_space=pltpu.SMEM),
                  pl.BlockSpec((N_ATTN_GROUPS, ATTN_BLK, 2 * ATTN_BLK), lambda b, hd, n: (0, 0, 0))] + qkv_specs,
        out_specs=pl.BlockSpec((1, ATTN_SPAN, HEAD_DIM), lambda b, hd, n: (b, n, hd)),
        out_shape=jax.ShapeDtypeStruct((BATCH, SEQ, ATTN_OUT), BF16),
        scratch_shapes=[pltpu.VMEM(scr_shape, F32), pltpu.VMEM(scr_shape, F32), pltpu.VMEM(bias_shape, F32),
                        pltpu.VMEM(stage_shape, F32), pltpu.VMEM(stage_shape, BF16),
                        pltpu.VMEM((SUB_BLOCKS, ATTN_BLK, HEAD_DIM), F32)],
        compiler_params=_params(("arbitrary", "arbitrary", "arbitrary"), blocks,
                                extra=2 * _nbytes(scr_shape, F32) + _nbytes(bias_shape, F32)
                                + _nbytes(stage_shape, F32) + _nbytes(stage_shape, BF16)
                                + _nbytes((SUB_BLOCKS, ATTN_BLK, HEAD_DIM), F32)),
        name="dilated_attn",
    )(rel_bias, jnp.asarray(_bucket_maps()), *([qkv] * len(qkv_specs)))


N_CHUNKS = SEQ // CHUNK
HEAD_PAIRS = HEADS_PER_SSM_GROUP // 2


def _split3(v):
    hi = v.astype(BF16)
    r1 = v - hi.astype(F32)
    mid = r1.astype(BF16)
    lo = (r1 - mid.astype(F32)).astype(BF16)
    return hi, mid, lo


def _ssd_kernel(xs_ref, bm_ref, cm_ref, z_ref, dtt_ref, cw_ref, cb_ref, acol_ref, dskip_ref, nw_ref,
                y_ref, state_ref):
    state_ref[...] = jnp.zeros_like(state_ref)

    li = lax.broadcasted_iota(jnp.int32, (CHUNK, CHUNK), 0)
    si = lax.broadcasted_iota(jnp.int32, (CHUNK, CHUNK), 1)
    tril = li >= si
    ltri = jnp.where(tril, 1.0, 0.0).astype(BF16)
    utri = jnp.where(li <= si, 1.0, 0.0).astype(BF16)
    low_half = si < SSM_HEAD_DIM
    low_row = lax.broadcasted_iota(jnp.int32, (1, LANES), 1) < SSM_HEAD_DIM
    nt = (((1,), (1,)), ((), ()))
    ri = lax.broadcasted_iota(jnp.int32, (CONV_K * CHUNK, 2 * CHUNK), 0)
    ci = lax.broadcasted_iota(jnp.int32, (CONV_K * CHUNK, 2 * CHUNK), 1)
    shift = jnp.where(ci == CHUNK + (ri & (CHUNK - 1)) - (ri >> 7), 1.0, 0.0).astype(BF16)

    cw = cw_ref[0]
    cb = cb_ref[0]
    a_col = -jnp.exp(acol_ref[0]) * LOG2_E
    dskip = dskip_ref[0]
    nw = nw_ref[...]

    def load(rows):
        return jnp.concatenate([xs_ref[0, rows, :], bm_ref[0, rows, :], cm_ref[0, rows, :]], axis=1)

    def chunk(c, carry):
        r0 = pl.multiple_of(c * CHUNK, CHUNK)
        rows = pl.ds(r0, CHUNK)
        u_prev = load(pl.ds(pl.multiple_of(jnp.maximum(r0 - CHUNK, 0), CHUNK), CHUNK))
        u_prev = jnp.where(c > 0, u_prev, jnp.zeros_like(u_prev))
        taps = jnp.dot(shift, jnp.concatenate([u_prev, load(rows)], axis=0), preferred_element_type=F32)
        acc = cb + cw[CONV_K - 1:CONV_K, :] * taps[0:CHUNK]
        for s in range(1, CONV_K):
            acc = acc + cw[CONV_K - 1 - s:CONV_K - s, :] * taps[s * CHUNK:(s + 1) * CHUNK]
        xbc = _silu(acc)
        x = xbc[:, :GROUP_W]
        bmat = xbc[:, GROUP_W:GROUP_W + D_STATE]
        cmat = xbc[:, GROUP_W + D_STATE:]

        dtt = dtt_ref[0, :, rows]
        dat3 = _split3(dtt * a_col)
        acum_t = sum(jnp.dot(p, utri, preferred_element_type=F32) for p in dat3)
        acum = sum(lax.dot_general(ltri, p, nt, preferred_element_type=F32) for p in dat3)
        w_t = dtt * jnp.exp2(acum_t[:, CHUNK - 1:CHUNK] - acum_t)
        cdec = jnp.exp2(acum[CHUNK - 1:CHUNK, :])

        bt = bmat.T
        cbm = jnp.dot(cmat.astype(BF16), bt.astype(BF16), preferred_element_type=F32)
        x_b = x.astype(BF16)

        y_blocks = []
        for k in range(HEAD_PAIRS):
            cols = slice(k * LANES, (k + 1) * LANES)
            xk = x_b[:, cols]
            zero = jnp.zeros_like(xk)
            x_lo, x_hi = jnp.where(low_half, xk, zero), jnp.where(low_half, zero, xk)
            st = state_ref[:, cols]
            st_b = st.astype(BF16)
            st_lo, st_hi = jnp.where(low_half, st_b, zero), jnp.where(low_half, zero, st_b)
            ms, cs, bts, cds = [], [], [], []
            for r in (2 * k, 2 * k + 1):
                col = jnp.broadcast_to(acum[:, r:r + 1], (CHUNK, CHUNK))
                row = jnp.broadcast_to(acum_t[r:r + 1, :], (CHUNK, CHUNK))
                ldec = jnp.exp2(jnp.where(tril, col - row, -jnp.inf))
                ms.append((cbm * ldec * jnp.broadcast_to(dtt[r:r + 1, :], (CHUNK, CHUNK))).astype(BF16))
                cs.append((cmat * jnp.exp2(col)).astype(BF16))
                bts.append((bt * jnp.broadcast_to(w_t[r:r + 1, :], (CHUNK, CHUNK))).astype(BF16))
                cds.append(jnp.broadcast_to(cdec[:, r:r + 1], (1, LANES)))
            y_blocks.append(jnp.dot(jnp.concatenate(ms + cs, axis=1),
                                    jnp.concatenate([x_lo, x_hi, st_lo, st_hi], axis=0),
                                    preferred_element_type=F32))
            upd = jnp.dot(jnp.concatenate(bts, axis=1), jnp.concatenate([x_lo, x_hi], axis=0),
                          preferred_element_type=F32)
            state_ref[:, cols] = st * jnp.where(low_row, cds[0], cds[1]) + upd
        y = jnp.concatenate(y_blocks, axis=1)

        y = y + dskip * x
        y = y * _silu(z_ref[0, rows, :].astype(F32))
        y = y * lax.rsqrt(jnp.mean(y * y, axis=-1, keepdims=True) + EPS)
        y_ref[0, rows, :] = (y * nw).astype(y_ref.dtype)
        return carry

    lax.fori_loop(0, N_CHUNKS, chunk, 0, unroll=4)


def _ssd(proj, dtt, conv_w_g, conv_b_g, a_col, dskip_x, norm_w):
    view = proj.reshape(BATCH, SEQ, PROJ_W)
    gw, ns = GROUP_W, D_STATE
    blocks = [((SEQ, gw), BF16)] * 3 + [((SEQ, ns), BF16)] * 2 + [((HEADS_PER_SSM_GROUP, SEQ), F32)]
    return pl.pallas_call(
        _ssd_kernel,
        grid=(BATCH, SSM_GROUPS),
        in_specs=[pl.BlockSpec((1, SEQ, gw), lambda b, g: (b, 0, P_XBC // gw + g)),
                  pl.BlockSpec((1, SEQ, ns), lambda b, g: (b, 0, P_B // ns + g)),
                  pl.BlockSpec((1, SEQ, ns), lambda b, g: (b, 0, P_C // ns + g)),
                  pl.BlockSpec((1, SEQ, gw), lambda b, g: (b, 0, P_Z // gw + g)),
                  pl.BlockSpec((1, HEADS_PER_SSM_GROUP, SEQ), lambda b, g: (b, g, 0)),
                  pl.BlockSpec((1, CONV_K, CONV_GROUP_W), lambda b, g: (g, 0, 0)),
                  pl.BlockSpec((1, 1, CONV_GROUP_W), lambda b, g: (g, 0, 0)),
                  pl.BlockSpec((1, HEADS_PER_SSM_GROUP, LANES), lambda b, g: (g, 0, 0)),
                  pl.BlockSpec((1, 1, gw), lambda b, g: (g, 0, 0)),
                  pl.BlockSpec((1, gw), lambda b, g: (0, g))],
        out_specs=pl.BlockSpec((1, SEQ, gw), lambda b, g: (b, 0, g)),
        out_shape=jax.ShapeDtypeStruct((BATCH, SEQ, D_INNER), BF16),
        scratch_shapes=[pltpu.VMEM((ns, gw), F32)],
        compiler_params=_params(("arbitrary", "arbitrary"), blocks, extra=_nbytes((ns, gw), F32) + (8 << 20)),
        name="ssd_scan",
    )(view, view, view, view, dtt, conv_w_g, conv_b_g, a_col, dskip_x, norm_w.reshape(1, D_INNER))


MG_TM = 1024
MG_TN = 512


def _merge_kernel(attn_ref, y_ref, ga_ref, gs_ref, wa_ref, ws_ref, out_ref):
    pa = jnp.dot(attn_ref[...], wa_ref[0], preferred_element_type=F32)
    ps = jnp.dot(y_ref[...], ws_ref[0], preferred_element_type=F32)
    merged = _sigmoid(ga_ref[...].astype(F32)) * pa + _sigmoid(gs_ref[...].astype(F32)) * ps
    out_ref[...] = merged.astype(out_ref.dtype)


def _merge_proj(attn, y, gates, w_attn, w_ssm, layer):
    row = lambda i, j: (i, 0)
    blocks = [((MG_TM, ATTN_OUT), BF16), ((MG_TM, D_INNER), BF16)] + [((MG_TM, MG_TN), BF16)] * 3 \
        + [((ATTN_OUT, MG_TN), BF16), ((D_INNER, MG_TN), BF16)]
    return pl.pallas_call(
        _merge_kernel,
        grid=(TOKENS // MG_TM, D_MODEL // MG_TN),
        in_specs=[pl.BlockSpec((MG_TM, ATTN_OUT), row),
                  pl.BlockSpec((MG_TM, D_INNER), row),
                  pl.BlockSpec((MG_TM, MG_TN), lambda i, j: (i, j)),
                  pl.BlockSpec((MG_TM, MG_TN), lambda i, j: (i, D_MODEL // MG_TN + j)),
                  pl.BlockSpec((1, ATTN_OUT, MG_TN), lambda i, j: (layer, 0, j)),
                  pl.BlockSpec((1, D_INNER, MG_TN), lambda i, j: (layer, 0, j))],
        out_specs=pl.BlockSpec((MG_TM, MG_TN), lambda i, j: (i, j)),
        out_shape=jax.ShapeDtypeStruct((TOKENS, D_MODEL), BF16),
        compiler_params=_params(("arbitrary", "arbitrary"), blocks, extra=4 * _nbytes((MG_TM, MG_TN), F32)),
        name="merge_proj",
    )(attn, y, gates, gates, w_attn, w_ssm)


def _resid_kernel(a_ref, w_ref, x_ref, g_ref, o_ref):
    o_ref[...] = x_ref[...] + g_ref[0] * jnp.dot(a_ref[...], w_ref[0], preferred_element_type=F32)


def _resid_proj(a, w, layer, x, gate, name):
    k = a.shape[1]
    tm, tn = 1024, 512
    blocks = [((tm, k), BF16), ((k, tn), BF16), ((tm, tn), F32), ((tm, tn), F32)]
    return pl.pallas_call(
        _resid_kernel,
        grid=(TOKENS // tm, D_MODEL // tn),
        in_specs=[pl.BlockSpec((tm, k), lambda i, j: (i, 0)),
                  pl.BlockSpec((1, k, tn), lambda i, j: (layer, 0, j)),
                  pl.BlockSpec((tm, tn), lambda i, j: (i, j)),
                  pl.BlockSpec((1, 1, tn), lambda i, j: (i * tm // SEQ, 0, j))],
        out_specs=pl.BlockSpec((tm, tn), lambda i, j: (i, j)),
        out_shape=jax.ShapeDtypeStruct((TOKENS, D_MODEL), F32),
        compiler_params=_params(("arbitrary", "arbitrary"), blocks, extra=_nbytes((tm, tn), F32)),
        name=name,
    )(a, w, x, gate)


RN_TM = 512


def _resid_norm_kernel(a_ref, w_ref, x_ref, g_ref, nw_ref, sc_ref, sh_ref, o_ref, h_ref):
    x1 = x_ref[...] + g_ref[0] * jnp.dot(a_ref[...], w_ref[0], preferred_element_type=F32)
    o_ref[...] = x1
    y = x1 * lax.rsqrt(jnp.mean(x1 * x1, axis=-1, keepdims=True) + EPS)
    y = y * nw_ref[...]
    h_ref[...] = (y * (1.0 + sc_ref[0]) + sh_ref[0]).astype(h_ref.dtype)


def _resid_norm_proj(a, w, layer, x, gate, norm_w, sc, sh):
    tm = RN_TM
    row = lambda i: (i, 0)
    per_batch = lambda i: (i * tm // SEQ, 0, 0)
    blocks = [((tm, D_MODEL), BF16), ((D_MODEL, D_MODEL), BF16), ((tm, D_MODEL), F32), ((tm, D_MODEL), F32),
              ((tm, D_MODEL), BF16)]
    return pl.pallas_call(
        _resid_norm_kernel,
        grid=(TOKENS // tm,),
        in_specs=[pl.BlockSpec((tm, D_MODEL), row),
                  pl.BlockSpec((1, D_MODEL, D_MODEL), lambda i: (layer, 0, 0)),
                  pl.BlockSpec((tm, D_MODEL), row),
                  pl.BlockSpec((1, 1, D_MODEL), per_batch),
                  pl.BlockSpec((1, D_MODEL), lambda i: (0, 0)),
                  pl.BlockSpec((1, 1, D_MODEL), per_batch),
                  pl.BlockSpec((1, 1, D_MODEL), per_batch)],
        out_specs=[pl.BlockSpec((tm, D_MODEL), row), pl.BlockSpec((tm, D_MODEL), row)],
        out_shape=[jax.ShapeDtypeStruct((TOKENS, D_MODEL), F32), jax.ShapeDtypeStruct((TOKENS, D_MODEL), BF16)],
        compiler_params=_params(("arbitrary",), blocks, extra=2 * _nbytes((tm, D_MODEL), F32)),
        name="out_proj_resid_norm",
    )(a, w, x, gate, norm_w.reshape(1, D_MODEL), sc, sh)


FF_TM = 1024
FF_TN = 512


def _swiglu_kernel(h_ref, wg_ref, wu_ref, o_ref):
    h = h_ref[...]
    hg = jnp.dot(h, wg_ref[0], preferred_element_type=F32)
    hu = jnp.dot(h, wu_ref[0], preferred_element_type=F32)
    o_ref[...] = (_silu(hg) * hu).astype(o_ref.dtype)


def _swiglu_in(h, w, layer):
    nblk = D_FF // FF_TN
    blocks = [((FF_TM, D_MODEL), BF16), ((D_MODEL, FF_TN), BF16), ((D_MODEL, FF_TN), BF16), ((FF_TM, FF_TN), BF16)]
    return pl.pallas_call(
        _swiglu_kernel,
        grid=(TOKENS // FF_TM, nblk),
        in_specs=[pl.BlockSpec((FF_TM, D_MODEL), lambda i, j: (i, 0)),
                  pl.BlockSpec((1, D_MODEL, FF_TN), lambda i, j: (layer, 0, j)),
                  pl.BlockSpec((1, D_MODEL, FF_TN), lambda i, j: (layer, 0, nblk + j))],
        out_specs=pl.BlockSpec((FF_TM, FF_TN), lambda i, j: (i, j)),
        out_shape=jax.ShapeDtypeStruct((TOKENS, D_FF), BF16),
        compiler_params=_params(("arbitrary", "arbitrary"), blocks, extra=3 * _nbytes((FF_TM, FF_TN), F32)),
        name="swiglu_in",
    )(h, w, w)


def _mixer(x, h, l, rel_bias, w_in, w_in_b, conv_w, conv_b, dt_bias, a_log, d_skip, ssm_norm_w,
           w_attn_b, w_ssm_b, w_out_b, gate, norm2):
    h2d = h.reshape(TOKENS, D_MODEL)
    qkv = _qkv_proj(h, w_in_b, l)
    proj = _matmul(h2d, w_in_b, l, OFF_Z, PROJ_W, 2048, 512, "in_proj")
    w_gates = w_in_b[l:l + 1, :, OFF_GA:]
    gates = _matmul(h2d, w_gates, 0, 0, 2 * D_MODEL, 2048, 512, "gate_proj")
    w_dt_t = w_in[l, :, OFF_DT:OFF_GA].T.astype(BF16)
    dtt = _dt_proj(h, w_dt_t, dt_bias[l])

    attn = _attention(qkv, rel_bias)

    cw, cbias = conv_w[l], conv_b[l]
    xs_w = cw[:, :D_INNER].reshape(CONV_K, SSM_GROUPS, GROUP_W)
    b_w = cw[:, D_INNER:D_INNER + SSM_GROUPS * D_STATE].reshape(CONV_K, SSM_GROUPS, D_STATE)
    c_w = cw[:, D_INNER + SSM_GROUPS * D_STATE:].reshape(CONV_K, SSM_GROUPS, D_STATE)
    conv_w_g = jnp.concatenate([xs_w, b_w, c_w], axis=2).transpose(1, 0, 2)
    conv_b_g = jnp.concatenate([cbias[:D_INNER].reshape(SSM_GROUPS, GROUP_W),
                                cbias[D_INNER:D_INNER + SSM_GROUPS * D_STATE].reshape(SSM_GROUPS, D_STATE),
                                cbias[D_INNER + SSM_GROUPS * D_STATE:].reshape(SSM_GROUPS, D_STATE)],
                               axis=1).reshape(SSM_GROUPS, 1, CONV_GROUP_W)
    a_col = jnp.broadcast_to(a_log[l].reshape(SSM_GROUPS, HEADS_PER_SSM_GROUP, 1),
                             (SSM_GROUPS, HEADS_PER_SSM_GROUP, LANES))
    dskip_x = jnp.repeat(d_skip[l], SSM_HEAD_DIM).reshape(SSM_GROUPS, 1, GROUP_W)
    y = _ssd(proj, dtt, conv_w_g, conv_b_g, a_col, dskip_x, ssm_norm_w[l])

    merged = _merge_proj(attn.reshape(TOKENS, ATTN_OUT), y.reshape(TOKENS, D_INNER), gates, w_attn_b, w_ssm_b, l)
    return _resid_norm_proj(merged, w_out_b, l, x, gate, *norm2)


def kernel(x, c, rel_bias, norm1_w, norm2_w, w_mod, b_mod, w_in, conv_w, conv_b, dt_bias, a_log, d_skip,
           ssm_norm_w, w_attn_proj, w_ssm_proj, w_out, w_ffn_in, w_ffn_out, final_norm_w):
    mod = _modulation(c, w_mod, b_mod)[:, :BATCH]
    xt = x.reshape(TOKENS, D_MODEL)
    w_in_b, w_attn_b, w_ssm_b, w_out_b, w_ffn_in_b, w_ffn_out_b = [
        w.astype(BF16) for w in (w_in, w_attn_proj, w_ssm_proj, w_out, w_ffn_in, w_ffn_out)]
    for l in range(DEPTH):
        sh1, sc1, g1, sh2, sc2, g2 = [m.reshape(BATCH, 1, D_MODEL) for m in jnp.split(mod[l], 6, axis=-1)]
        h = _norm(xt.reshape(BATCH, SEQ, D_MODEL), norm1_w[l], sc1, sh1, BF16, True)
        xt, h = _mixer(xt, h, l, rel_bias, w_in, w_in_b, conv_w, conv_b, dt_bias, a_log, d_skip, ssm_norm_w,
                       w_attn_b, w_ssm_b, w_out_b, g1, (norm2_w[l], sc2, sh2))
        u = _swiglu_in(h, w_ffn_in_b, l)
        xt = _resid_proj(u, w_ffn_out_b, l, xt, g2, "ffn_out_resid")
    zeros = jnp.zeros((BATCH, 1, D_MODEL), F32)
    return _norm(xt.reshape(BATCH, SEQ, D_MODEL), final_norm_w, zeros, zeros, F32, False)
```

```python
import functools
import math

import jax
import jax.numpy as jnp
import numpy as np
from jax import lax
from jax.experimental import pallas as pl
from jax.experimental.pallas import tpu as pltpu

F32 = jnp.float32
BF16 = jnp.bfloat16

D_MODEL = 2048
BATCH = 4
SEQ = 4096
TOKENS = BATCH * SEQ
DEPTH = 2
DILATED_GROUPS = ((128, 1), (512, 4), (2048, 16))
N_ATTN_GROUPS = len(DILATED_GROUPS)
HEADS_PER_GROUP = 4
N_ATTN_HEADS = N_ATTN_GROUPS * HEADS_PER_GROUP
HEAD_DIM = 128
QKV_W = N_ATTN_HEADS * HEAD_DIM
ATTN_OUT = HEADS_PER_GROUP * HEAD_DIM
ATTN_BLK = 128
REL_BUCKETS = 32
REL_MAX_DIST = 2048
D_INNER = 2 * D_MODEL
SSM_HEAD_DIM = 64
SSM_HEADS = D_INNER // SSM_HEAD_DIM
SSM_GROUPS = 8
HEADS_PER_SSM_GROUP = SSM_HEADS // SSM_GROUPS
GROUP_W = HEADS_PER_SSM_GROUP * SSM_HEAD_DIM
D_STATE = 128
CONV_K = 4
CHUNK = 128
CONV_DIM = D_INNER + 2 * SSM_GROUPS * D_STATE
CONV_GROUP_W = GROUP_W + 2 * D_STATE
D_FF = 5632
SPLITS = (QKV_W, QKV_W, QKV_W, D_INNER, CONV_DIM, SSM_HEADS, D_MODEL, D_MODEL)
OFF_Q, OFF_K, OFF_V, OFF_Z, OFF_XBC, OFF_DT, OFF_GA, OFF_GS = np.cumsum((0,) + SPLITS[:-1]).tolist()
N_QKV_HEADS = 3 * N_ATTN_HEADS
PROJ_W = D_INNER + CONV_DIM + 2 * D_MODEL
P_Z = 0
P_XBC = D_INNER
P_B = P_XBC + D_INNER
P_C = P_B + SSM_GROUPS * D_STATE
P_GA = P_XBC + CONV_DIM
P_GS = P_GA + D_MODEL
EPS = 1e-6
LOG2_E = math.log2(math.e)

LANES = 128
VMEM_LIMIT_CAP = 60 * 1024 * 1024


def _nbytes(shape, dtype):
    return int(np.prod(shape)) * jnp.dtype(dtype).itemsize


def _params(sem, blocks, extra=0):
    need = 2 * sum(_nbytes(s, d) for s, d in blocks) + extra
    return pltpu.CompilerParams(dimension_semantics=sem,
                                vmem_limit_bytes=min(VMEM_LIMIT_CAP, need + (8 << 20)))


def _sigmoid(v):
    return 0.5 + 0.5 * jnp.tanh(0.5 * v)


def _silu(v):
    h = 0.5 * v
    return h + h * jnp.tanh(h)


MOD_TN = 1024
MOD_ROWS = 8


def _mod_kernel(c_ref, w_ref, b_ref, o_ref):
    ca = _silu(c_ref[...])
    o_ref[0] = jnp.dot(ca, w_ref[0], preferred_element_type=F32) + b_ref[0]


def _modulation(c, w_mod, b_mod):
    c8 = jnp.zeros((MOD_ROWS, D_MODEL), F32).at[:BATCH].set(c)
    n = 6 * D_MODEL
    blocks = [((MOD_ROWS, D_MODEL), F32), ((D_MODEL, MOD_TN), F32), ((1, MOD_TN), F32), ((MOD_ROWS, MOD_TN), F32)]
    return pl.pallas_call(
        _mod_kernel,
        grid=(DEPTH, n // MOD_TN),
        in_specs=[pl.BlockSpec((MOD_ROWS, D_MODEL), lambda l, j: (0, 0)),
                  pl.BlockSpec((1, D_MODEL, MOD_TN), lambda l, j: (l, 0, j)),
                  pl.BlockSpec((1, 1, MOD_TN), lambda l, j: (l, 0, j))],
        out_specs=pl.BlockSpec((1, MOD_ROWS, MOD_TN), lambda l, j: (l, 0, j)),
        out_shape=jax.ShapeDtypeStruct((DEPTH, MOD_ROWS, n), F32),
        compiler_params=_params(("arbitrary", "arbitrary"), blocks),
        name="modulation",
    )(c8, w_mod, b_mod.reshape(DEPTH, 1, n))


NORM_TS = 1024


def _norm_kernel(x_ref, w_ref, sc_ref, sh_ref, o_ref, *, modulate):
    x = x_ref[0]
    y = x * lax.rsqrt(jnp.mean(x * x, axis=-1, keepdims=True) + EPS)
    y = y * w_ref[...]
    if modulate:
        y = y * (1.0 + sc_ref[0]) + sh_ref[0]
    o_ref[0] = y.astype(o_ref.dtype)


def _norm(x, w, sc, sh, out_dtype, modulate):
    blocks = [((NORM_TS, D_MODEL), F32), ((NORM_TS, D_MODEL), out_dtype)]
    return pl.pallas_call(
        functools.partial(_norm_kernel, modulate=modulate),
        grid=(BATCH, SEQ // NORM_TS),
        in_specs=[pl.BlockSpec((1, NORM_TS, D_MODEL), lambda b, i: (b, i, 0)),
                  pl.BlockSpec((1, D_MODEL), lambda b, i: (0, 0)),
                  pl.BlockSpec((1, 1, D_MODEL), lambda b, i: (b, 0, 0)),
                  pl.BlockSpec((1, 1, D_MODEL), lambda b, i: (b, 0, 0))],
        out_specs=pl.BlockSpec((1, NORM_TS, D_MODEL), lambda b, i: (b, i, 0)),
        out_shape=jax.ShapeDtypeStruct((BATCH, SEQ, D_MODEL), out_dtype),
        compiler_params=_params(("arbitrary", "arbitrary"), blocks, extra=3 * _nbytes((NORM_TS, D_MODEL), F32)),
        name="rmsnorm_mod" if modulate else "rmsnorm",
    )(x, w.reshape(1, D_MODEL), sc, sh)


IN_TM = 2048
IN_TN = 512


def _in_proj_kernel(a_ref, w0_ref, w1_ref, o_ref, *, n0_tiles):
    j = pl.program_id(1)

    @pl.when(j < n0_tiles)
    def _():
        o_ref[...] = jnp.dot(a_ref[...], w0_ref[0], preferred_element_type=F32).astype(o_ref.dtype)

    @pl.when(j >= n0_tiles)
    def _():
        o_ref[...] = jnp.dot(a_ref[...], w1_ref[0], preferred_element_type=F32).astype(o_ref.dtype)


def _in_proj(a, w0, layer, col0, n0, w1):
    m, k = a.shape
    n1 = w1.shape[2]
    tm, tn = IN_TM, IN_TN
    n0_tiles = n0 // tn
    blocks = [((tm, k), BF16), ((k, tn), BF16), ((k, tn), BF16), ((tm, tn), BF16)]
    return pl.pallas_call(
        functools.partial(_in_proj_kernel, n0_tiles=n0_tiles),
        grid=(m // tm, (n0 + n1) // tn),
        in_specs=[pl.BlockSpec((tm, k), lambda i, j: (i, 0)),
                  pl.BlockSpec((1, k, tn), lambda i, j: (layer, 0, col0 // tn + jnp.minimum(j, n0_tiles - 1))),
                  pl.BlockSpec((1, k, tn), lambda i, j: (0, 0, jnp.maximum(j - n0_tiles, 0)))],
        out_specs=pl.BlockSpec((tm, tn), lambda i, j: (i, j)),
        out_shape=jax.ShapeDtypeStruct((m, n0 + n1), BF16),
        compiler_params=_params(("arbitrary", "arbitrary"), blocks, extra=_nbytes((tm, tn), F32)),
        name="in_proj",
    )(a, w0, w1)


QKV_TM = 2048
QKV_HEADS_PER_STEP = 6


def _qkv_kernel(h_ref, w_ref, o_ref):
    res = jnp.dot(h_ref[0], w_ref[0], preferred_element_type=F32)
    for hd in range(QKV_HEADS_PER_STEP):
        o_ref[0, hd] = res[:, hd * HEAD_DIM:(hd + 1) * HEAD_DIM]


def _qkv_proj(h, w, layer):
    tn = QKV_HEADS_PER_STEP * HEAD_DIM
    blocks = [((QKV_TM, D_MODEL), BF16), ((D_MODEL, tn), BF16), ((QKV_TM, tn), F32)]
    return pl.pallas_call(
        _qkv_kernel,
        grid=(BATCH, SEQ // QKV_TM, N_QKV_HEADS // QKV_HEADS_PER_STEP),
        in_specs=[pl.BlockSpec((1, QKV_TM, D_MODEL), lambda b, i, j: (b, i, 0)),
                  pl.BlockSpec((1, D_MODEL, tn), lambda b, i, j: (layer, 0, j))],
        out_specs=pl.BlockSpec((1, QKV_HEADS_PER_STEP, QKV_TM, HEAD_DIM), lambda b, i, j: (b, j, i, 0)),
        out_shape=jax.ShapeDtypeStruct((BATCH, N_QKV_HEADS, SEQ, HEAD_DIM), F32),
        compiler_params=_params(("arbitrary", "arbitrary", "arbitrary"), blocks, extra=_nbytes((QKV_TM, tn), F32)),
        name="qkv_proj",
    )(h, w)


DT_TS = 1024


def _dt_kernel(h_ref, wt_ref, bias_ref, o_ref):
    raw = lax.dot_general(wt_ref[...], h_ref[0], (((1,), (1,)), ((), ())), preferred_element_type=F32)
    v = raw + bias_ref[...]
    o_ref[0] = jnp.maximum(v, 0.0) + jnp.log1p(jnp.exp(-jnp.abs(v)))


def _dt_proj(h, w_dt_t, dt_bias):
    blocks = [((DT_TS, D_MODEL), BF16), ((SSM_HEADS, D_MODEL), BF16), ((SSM_HEADS, DT_TS), F32)]
    return pl.pallas_call(
        _dt_kernel,
        grid=(BATCH, SEQ // DT_TS),
        in_specs=[pl.BlockSpec((1, DT_TS, D_MODEL), lambda b, i: (b, i, 0)),
                  pl.BlockSpec((SSM_HEADS, D_MODEL), lambda b, i: (0, 0)),
                  pl.BlockSpec((SSM_HEADS, 1), lambda b, i: (0, 0))],
        out_specs=pl.BlockSpec((1, SSM_HEADS, DT_TS), lambda b, i: (b, 0, i)),
        out_shape=jax.ShapeDtypeStruct((BATCH, SSM_HEADS, SEQ), F32),
        compiler_params=_params(("arbitrary", "arbitrary"), blocks),
        name="dt_proj",
    )(h, w_dt_t, dt_bias.reshape(SSM_HEADS, 1))


ATTN_SPAN = 2048
SUB_BLOCKS = ATTN_SPAN // ATTN_BLK
MERGE_ROWS = 256


def _bucket_maps():
    qi = np.arange(ATTN_BLK)[:, None]
    kj = np.arange(2 * ATTN_BLK)[None, :]
    steps = np.clip(ATTN_BLK + qi - kj, 0, ATTN_BLK)
    exact = REL_BUCKETS // 2
    maps = []
    for _, dil in DILATED_GROUPS:
        dist = steps * dil
        n = np.maximum(dist, 1).astype(np.float32)
        large = exact + (np.log(n / np.float32(exact)) / np.float32(math.log(REL_MAX_DIST / exact))
                         * np.float32(REL_BUCKETS - exact)).astype(np.int32)
        large = np.minimum(large, REL_BUCKETS - 1)
        maps.append(np.where(dist < exact, dist, large).astype(np.int32))
    return np.stack(maps)


def _attn_kernel(tab_ref, bucket_ref, *refs):
    qkv_refs = refs[:5 * N_ATTN_GROUPS]
    out_ref, o_scr, l_scr, bias_ref, s_scr, p_scr, d_scr = refs[5 * N_ATTN_GROUPS:]
    b, hd, n = pl.program_id(0), pl.program_id(1), pl.program_id(2)

    kj2 = lax.broadcasted_iota(jnp.int32, (ATTN_BLK, 2 * ATTN_BLK), 1)
    qi2 = lax.broadcasted_iota(jnp.int32, (ATTN_BLK, 2 * ATTN_BLK), 0)

    @pl.when((b == 0) & (n == 0))
    def _():
        band = (kj2 >= qi2) & (kj2 <= qi2 + ATTN_BLK)
        for g in range(N_ATTN_GROUPS):
            bucket = bucket_ref[g]
            bias = jnp.zeros((ATTN_BLK, 2 * ATTN_BLK), F32)
            for k in range(REL_BUCKETS):
                bias = jnp.where(bucket == k, tab_ref[k, g * HEADS_PER_GROUP + hd], bias)
            bias_ref[g * HEADS_PER_GROUP + hd] = jnp.where(band, bias * LOG2_E, -jnp.inf)

    has_prev = (kj2 >= ATTN_BLK) | (n > 0)
    scale = LOG2_E / math.sqrt(HEAD_DIM)
    nt = (((1,), (1,)), ((), ()))

    for g, (_, dil) in enumerate(DILATED_GROUPS):
        q_ref, kc_ref, kp_ref, vc_ref, vp_ref = qkv_refs[5 * g:5 * g + 5]
        bias = bias_ref[g * HEADS_PER_GROUP + hd]
        bias_first = jnp.where(has_prev, bias, -jnp.inf)
        per_residue = SUB_BLOCKS // dil

        def rows(r, m, dil=dil):
            start = r + m * ATTN_BLK * dil
            return pl.ds(start, ATTN_BLK, stride=dil) if dil > 1 else pl.ds(start, ATTN_BLK)

        def both(prev_ref, cur_ref, r, m):
            prev = prev_ref[0, 0, rows(r, per_residue - 1), :] if m == 0 else cur_ref[0, 0, rows(r, m - 1), :]
            return jnp.concatenate([prev.astype(BF16), cur_ref[0, 0, rows(r, m), :].astype(BF16)], axis=0)

        subs = [(r, m) for r in range(dil) for m in range(per_residue)]
        for i, (r, m) in enumerate(subs):
            q = q_ref[0, 0, rows(r, m), :].astype(BF16)
            s = lax.dot_general(q, both(kp_ref, kc_ref, r, m), nt, preferred_element_type=F32)
            s_scr[i] = s * scale + (bias_first if m == 0 else bias)
        for i, (r, m) in enumerate(subs):
            s = s_scr[i]
            mx = jnp.max(s, axis=-1, keepdims=True)
            p = jnp.exp2(s - mx)
            den = jnp.sum(p, axis=-1, keepdims=True)
            p_scr[i] = p.astype(BF16)
            d_scr[i] = jnp.broadcast_to(1.0 / den, (ATTN_BLK, HEAD_DIM))
            l_scr[g, rows(r, m), :] = jnp.broadcast_to(mx + jnp.log2(den), (ATTN_BLK, HEAD_DIM))
        for i, (r, m) in enumerate(subs):
            acc = jnp.dot(p_scr[i], both(vp_ref, vc_ref, r, m), preferred_element_type=F32)
            o_scr[g, rows(r, m), :] = acc * d_scr[i]

    for c in range(ATTN_SPAN // MERGE_ROWS):
        sl = pl.ds(c * MERGE_ROWS, MERGE_ROWS)
        l0, l1, l2 = l_scr[0, sl, :], l_scr[1, sl, :], l_scr[2, sl, :]
        mx = jnp.maximum(jnp.maximum(l0, l1), l2)
        e0, e1, e2 = jnp.exp2(l0 - mx), jnp.exp2(l1 - mx), jnp.exp2(l2 - mx)
        attn = (e0 * o_scr[0, sl, :] + e1 * o_scr[1, sl, :] + e2 * o_scr[2, sl, :]) / (e0 + e1 + e2)
        out_ref[0, sl, :] = attn.astype(out_ref.dtype)


def _attention(qkv, rel_bias):
    blk = (1, 1, ATTN_SPAN, HEAD_DIM)

    def spec(which, g, prev):
        base = which * N_ATTN_HEADS + g * HEADS_PER_GROUP
        if prev:
            return pl.BlockSpec(blk, lambda b, hd, n: (b, base + hd, jnp.maximum(n - 1, 0), 0))
        return pl.BlockSpec(blk, lambda b, hd, n: (b, base + hd, n, 0))

    qkv_specs = []
    for g in range(N_ATTN_GROUPS):
        qkv_specs += [spec(0, g, False), spec(1, g, False), spec(1, g, True), spec(2, g, False), spec(2, g, True)]
    bias_shape = (N_ATTN_HEADS, ATTN_BLK, 2 * ATTN_BLK)
    scr_shape = (N_ATTN_GROUPS, ATTN_SPAN, HEAD_DIM)
    stage_shape = (SUB_BLOCKS, ATTN_BLK, 2 * ATTN_BLK)
    blocks =[(blk, F32)] * len(qkv_specs) + [((ATTN_SPAN, HEAD_DIM), BF16), (bias_shape, jnp.int32)]
    return pl.pallas_call(
        _attn_kernel,
        grid=(BATCH, HEADS_PER_GROUP, SEQ // ATTN_SPAN),
        in_specs=[pl.BlockSpec(memory_space=pltpu.SMEM),
                  pl.BlockSpec((N_ATTN_GROUPS, ATTN_BLK, 2 * ATTN_BLK), lambda b, hd, n: (0, 0, 0))] + qkv_specs,
        out_specs=pl.BlockSpec((1, ATTN_SPAN, HEAD_DIM), lambda b, hd, n: (b, n, hd)),
        out_shape=jax.ShapeDtypeStruct((BATCH, SEQ, ATTN_OUT), BF16),
        scratch_shapes=[pltpu.VMEM(scr_shape, F32), pltpu.VMEM(scr_shape, F32), pltpu.VMEM(bias_shape, F32),
                        pltpu.VMEM(stage_shape, F32), pltpu.VMEM(stage_shape, BF16),
                        pltpu.VMEM((SUB_BLOCKS, ATTN_BLK, HEAD_DIM), F32)],
        compiler_params=_params(("arbitrary", "arbitrary", "arbitrary"), blocks,
                                extra=2 * _nbytes(scr_shape, F32) + _nbytes(bias_shape, F32)
                                + _nbytes(stage_shape, F32) + _nbytes(stage_shape, BF16)
                                + _nbytes((SUB_BLOCKS, ATTN_BLK, HEAD_DIM), F32)),
        name="dilated_attn",
    )(rel_bias, jnp.asarray(_bucket_maps()), *([qkv] * len(qkv_specs)))


N_CHUNKS = SEQ // CHUNK
HEAD_PAIRS = HEADS_PER_SSM_GROUP // 2


def _split3(v):
    hi = v.astype(BF16)
    r1 = v - hi.astype(F32)
    mid = r1.astype(BF16)
    lo = (r1 - mid.astype(F32)).astype(BF16)
    return hi, mid, lo


def _ssd_kernel(xs_ref, bm_ref, cm_ref, z_ref, dtt_ref, cw_ref, cb_ref, acol_ref, dskip_ref, nw_ref,
                y_ref, state_ref):
    state_ref[...] = jnp.zeros_like(state_ref)

    li = lax.broadcasted_iota(jnp.int32, (CHUNK, CHUNK), 0)
    si = lax.broadcasted_iota(jnp.int32, (CHUNK, CHUNK), 1)
    tril = li >= si
    ltri = jnp.where(tril, 1.0, 0.0).astype(BF16)
    utri = jnp.where(li <= si, 1.0, 0.0).astype(BF16)
    low_half = si < SSM_HEAD_DIM
    low_row = lax.broadcasted_iota(jnp.int32, (1, LANES), 1) < SSM_HEAD_DIM
    nt = (((1,), (1,)), ((), ()))
    ri = lax.broadcasted_iota(jnp.int32, (CONV_K * CHUNK, 2 * CHUNK), 0)
    ci = lax.broadcasted_iota(jnp.int32, (CONV_K * CHUNK, 2 * CHUNK), 1)
    shift = jnp.where(ci == CHUNK + (ri & (CHUNK - 1)) - (ri >> 7), 1.0, 0.0).astype(BF16)

    cw = cw_ref[0]
    cb = cb_ref[0]
    a_col = -jnp.exp(acol_ref[0]) * LOG2_E
    dskip = dskip_ref[0]
    nw = nw_ref[...]

    def load(rows):
        return jnp.concatenate([xs_ref[0, rows, :], bm_ref[0, rows, :], cm_ref[0, rows, :]], axis=1)

    def chunk(c, carry):
        r0 = pl.multiple_of(c * CHUNK, CHUNK)
        rows = pl.ds(r0, CHUNK)
        u_prev = load(pl.ds(pl.multiple_of(jnp.maximum(r0 - CHUNK, 0), CHUNK), CHUNK))
        u_prev = jnp.where(c > 0, u_prev, jnp.zeros_like(u_prev))
        taps = jnp.dot(shift, jnp.concatenate([u_prev, load(rows)], axis=0), preferred_element_type=F32)
        acc = cb + cw[CONV_K - 1:CONV_K, :] * taps[0:CHUNK]
        for s in range(1, CONV_K):
            acc = acc + cw[CONV_K - 1 - s:CONV_K - s, :] * taps[s * CHUNK:(s + 1) * CHUNK]
        xbc = _silu(acc)
        x = xbc[:, :GROUP_W]
        bmat = xbc[:, GROUP_W:GROUP_W + D_STATE]
        cmat = xbc[:, GROUP_W + D_STATE:]

        dtt = dtt_ref[0, :, rows]
        dat3 = _split3(dtt * a_col)
        acum_t = sum(jnp.dot(p, utri, preferred_element_type=F32) for p in dat3)
        acum = sum(lax.dot_general(ltri, p, nt, preferred_element_type=F32) for p in dat3)
        w_t = dtt * jnp.exp2(acum_t[:, CHUNK - 1:CHUNK] - acum_t)
        cdec = jnp.exp2(acum[CHUNK - 1:CHUNK, :])

        bt = bmat.T
        cbm = jnp.dot(cmat.astype(BF16), bt.astype(BF16), preferred_element_type=F32)
        x_b = x.astype(BF16)

        y_blocks = []
        for k in range(HEAD_PAIRS):
            cols = slice(k * LANES, (k + 1) * LANES)
            xk = x_b[:, cols]
            zero = jnp.zeros_like(xk)
            x_lo, x_hi = jnp.where(low_half, xk, zero), jnp.where(low_half, zero, xk)
            st = state_ref[:, cols]
            st_b = st.astype(BF16)
            st_lo, st_hi = jnp.where(low_half, st_b, zero), jnp.where(low_half, zero, st_b)
            ms, cs, bts, cds = [], [], [], []
            for r in (2 * k, 2 * k + 1):
                col = jnp.broadcast_to(acum[:, r:r + 1], (CHUNK, CHUNK))
                row = jnp.broadcast_to(acum_t[r:r + 1, :], (CHUNK, CHUNK))
                ldec = jnp.exp2(jnp.where(tril, col - row, -jnp.inf))
                ms.append((cbm * ldec * jnp.broadcast_to(dtt[r:r + 1, :], (CHUNK, CHUNK))).astype(BF16))
                cs.append((cmat * jnp.exp2(col)).astype(BF16))
                bts.append((bt * jnp.broadcast_to(w_t[r:r + 1, :], (CHUNK, CHUNK))).astype(BF16))
                cds.append(jnp.broadcast_to(cdec[:, r:r + 1], (1, LANES)))
            y_blocks.append(jnp.dot(jnp.concatenate(ms + cs, axis=1),
                                    jnp.concatenate([x_lo, x_hi, st_lo, st_hi], axis=0),
                                    preferred_element_type=F32))
            upd = jnp.dot(jnp.concatenate(bts, axis=1), jnp.concatenate([x_lo, x_hi], axis=0),
                          preferred_element_type=F32)
            state_ref[:, cols] = st * jnp.where(low_row, cds[0], cds[1]) + upd
        y = jnp.concatenate(y_blocks, axis=1)

        y = y + dskip * x
        y = y * _silu(z_ref[0, rows, :].astype(F32))
        y = y * lax.rsqrt(jnp.mean(y * y, axis=-1, keepdims=True) + EPS)
        y_ref[0, rows, :] = (y * nw).astype(y_ref.dtype)
        return carry

    lax.fori_loop(0, N_CHUNKS, chunk, 0, unroll=4)


def _ssd(proj, dtt, conv_w_g, conv_b_g, a_col, dskip_x, norm_w):
    view = proj.reshape(BATCH, SEQ, PROJ_W)
    gw, ns = GROUP_W, D_STATE
    blocks = [((SEQ, gw), BF16)] * 3 + [((SEQ, ns), BF16)] * 2 + [((HEADS_PER_SSM_GROUP, SEQ), F32)]
    return pl.pallas_call(
        _ssd_kernel,
        grid=(BATCH, SSM_GROUPS),
        in_specs=[pl.BlockSpec((1, SEQ, gw), lambda b, g: (b, 0, P_XBC // gw + g)),
                  pl.BlockSpec((1, SEQ, ns), lambda b, g: (b, 0, P_B // ns + g)),
                  pl.BlockSpec((1, SEQ, ns), lambda b, g: (b, 0, P_C // ns + g)),
                  pl.BlockSpec((1, SEQ, gw), lambda b, g: (b, 0, P_Z // gw + g)),
                  pl.BlockSpec((1, HEADS_PER_SSM_GROUP, SEQ), lambda b, g: (b, g, 0)),
                  pl.BlockSpec((1, CONV_K, CONV_GROUP_W), lambda b, g: (g, 0, 0)),
                  pl.BlockSpec((1, 1, CONV_GROUP_W), lambda b, g: (g, 0, 0)),
                  pl.BlockSpec((1, HEADS_PER_SSM_GROUP, LANES), lambda b, g: (g, 0, 0)),
                  pl.BlockSpec((1, 1, gw), lambda b, g: (g, 0, 0)),
                  pl.BlockSpec((1, gw), lambda b, g: (0, g))],
        out_specs=pl.BlockSpec((1, SEQ, gw), lambda b, g: (b, 0, g)),
        out_shape=jax.ShapeDtypeStruct((BATCH, SEQ, D_INNER), BF16),
        scratch_shapes=[pltpu.VMEM((ns, gw), F32)],
        compiler_params=_params(("arbitrary", "arbitrary"), blocks, extra=_nbytes((ns, gw), F32) + (8 << 20)),
        name="ssd_scan",
    )(view, view, view, view, dtt, conv_w_g, conv_b_g, a_col, dskip_x, norm_w.reshape(1, D_INNER))


MG_TM = 1024
MG_TN = 512


def _merge_kernel(attn_ref, y_ref, ga_ref, gs_ref, wa_ref, ws_ref, out_ref):
    pa = jnp.dot(attn_ref[...], wa_ref[0], preferred_element_type=F32)
    ps = jnp.dot(y_ref[...], ws_ref[0], preferred_element_type=F32)
    merged = _sigmoid(ga_ref[...].astype(F32)) * pa + _sigmoid(gs_ref[...].astype(F32)) * ps
    out_ref[...] = merged.astype(out_ref.dtype)


def _merge_proj(attn, y, gates, w_attn, w_ssm, layer):
    row = lambda i, j: (i, 0)
    blocks = [((MG_TM, ATTN_OUT), BF16), ((MG_TM, D_INNER), BF16)] + [((MG_TM, MG_TN), BF16)] * 3 \
        + [((ATTN_OUT, MG_TN), BF16), ((D_INNER, MG_TN), BF16)]
    return pl.pallas_call(
        _merge_kernel,
        grid=(TOKENS // MG_TM, D_MODEL // MG_TN),
        in_specs=[pl.BlockSpec((MG_TM, ATTN_OUT), row),
                  pl.BlockSpec((MG_TM, D_INNER), row),
                  pl.BlockSpec((MG_TM, MG_TN), lambda i, j: (i, P_GA // MG_TN + j)),
                  pl.BlockSpec((MG_TM, MG_TN), lambda i, j: (i, P_GS // MG_TN + j)),
                  pl.BlockSpec((1, ATTN_OUT, MG_TN), lambda i, j: (layer, 0, j)),
                  pl.BlockSpec((1, D_INNER, MG_TN), lambda i, j: (layer, 0, j))],
        out_specs=pl.BlockSpec((MG_TM, MG_TN), lambda i, j: (i, j)),
        out_shape=jax.ShapeDtypeStruct((TOKENS, D_MODEL), BF16),
        compiler_params=_params(("arbitrary", "arbitrary"), blocks, extra=4 * _nbytes((MG_TM, MG_TN), F32)),
        name="merge_proj",
    )(attn, y, gates, gates, w_attn, w_ssm)


def _resid_kernel(a_ref, w_ref, x_ref, g_ref, o_ref):
    o_ref[...] = x_ref[...] + g_ref[0] * jnp.dot(a_ref[...], w_ref[0], preferred_element_type=F32)


def _resid_proj(a, w, layer, x, gate, name):
    k = a.shape[1]
    tm, tn = 1024, 512
    blocks = [((tm, k), BF16), ((k, tn), BF16), ((tm, tn), F32), ((tm, tn), F32)]
    return pl.pallas_call(
        _resid_kernel,
        grid=(TOKENS // tm, D_MODEL // tn),
        in_specs=[pl.BlockSpec((tm, k), lambda i, j: (i, 0)),
                  pl.BlockSpec((1, k, tn), lambda i, j: (layer, 0, j)),
                  pl.BlockSpec((tm, tn), lambda i, j: (i, j)),
                  pl.BlockSpec((1, 1, tn), lambda i, j: (i * tm // SEQ, 0, j))],
        out_specs=pl.BlockSpec((tm, tn), lambda i, j: (i, j)),
        out_shape=jax.ShapeDtypeStruct((TOKENS, D_MODEL), F32),
        compiler_params=_params(("arbitrary", "arbitrary"), blocks, extra=_nbytes((tm, tn), F32)),
        name=name,
    )(a, w, x, gate)


RN_TM = 512


def _resid_norm_kernel(a_ref, w_ref, x_ref, g_ref, nw_ref, sc_ref, sh_ref, o_ref, h_ref):
    x1 = x_ref[...] + g_ref[0] * jnp.dot(a_ref[...], w_ref[0], preferred_element_type=F32)
    o_ref[...] = x1
    y = x1 * lax.rsqrt(jnp.mean(x1 * x1, axis=-1, keepdims=True) + EPS)
    y = y * nw_ref[...]
    h_ref[...] = (y * (1.0 + sc_ref[0]) + sh_ref[0]).astype(h_ref.dtype)


def _resid_norm_proj(a, w, layer, x, gate, norm_w, sc, sh):
    tm = RN_TM
    row = lambda i: (i, 0)
    per_batch = lambda i: (i * tm // SEQ, 0, 0)
    blocks = [((tm, D_MODEL), BF16), ((D_MODEL, D_MODEL), BF16), ((tm, D_MODEL), F32), ((tm, D_MODEL), F32),
              ((tm, D_MODEL), BF16)]
    return pl.pallas_call(
        _resid_norm_kernel,
        grid=(TOKENS // tm,),
        in_specs=[pl.BlockSpec((tm, D_MODEL), row),
                  pl.BlockSpec((1, D_MODEL, D_MODEL), lambda i: (layer, 0, 0)),
                  pl.BlockSpec((tm, D_MODEL), row),
                  pl.BlockSpec((1, 1, D_MODEL), per_batch),
                  pl.BlockSpec((1, D_MODEL), lambda i: (0, 0)),
                  pl.BlockSpec((1, 1, D_MODEL), per_batch),
                  pl.BlockSpec((1, 1, D_MODEL), per_batch)],
        out_specs=[pl.BlockSpec((tm, D_MODEL), row), pl.BlockSpec((tm, D_MODEL), row)],
        out_shape=[jax.ShapeDtypeStruct((TOKENS, D_MODEL), F32), jax.ShapeDtypeStruct((TOKENS, D_MODEL), BF16)],
        compiler_params=_params(("arbitrary",), blocks, extra=2 * _nbytes((tm, D_MODEL), F32)),
        name="out_proj_resid_norm",
    )(a, w, x, gate, norm_w.reshape(1, D_MODEL), sc, sh)


FF_TM = 1024
FF_TN = 512


def _swiglu_kernel(h_ref, wg_ref, wu_ref, o_ref):
    h = h_ref[...]
    hg = jnp.dot(h, wg_ref[0], preferred_element_type=F32)
    hu = jnp.dot(h, wu_ref[0], preferred_element_type=F32)
    o_ref[...] = (_silu(hg) * hu).astype(o_ref.dtype)


def _swiglu_in(h, w, layer):
    nblk = D_FF // FF_TN
    blocks = [((FF_TM, D_MODEL), BF16), ((D_MODEL, FF_TN), BF16), ((D_MODEL, FF_TN), BF16), ((FF_TM, FF_TN), BF16)]
    return pl.pallas_call(
        _swiglu_kernel,
        grid=(TOKENS // FF_TM, nblk),
        in_specs=[pl.BlockSpec((FF_TM, D_MODEL), lambda i, j: (i, 0)),
                  pl.BlockSpec((1, D_MODEL, FF_TN), lambda i, j: (layer, 0, j)),
                  pl.BlockSpec((1, D_MODEL, FF_TN), lambda i, j: (layer, 0, nblk + j))],
        out_specs=pl.BlockSpec((FF_TM, FF_TN), lambda i, j: (i, j)),
        out_shape=jax.ShapeDtypeStruct((TOKENS, D_FF), BF16),
        compiler_params=_params(("arbitrary", "arbitrary"), blocks, extra=3 * _nbytes((FF_TM, FF_TN), F32)),
        name="swiglu_in",
    )(h, w, w)


def _mixer(x, h, l, rel_bias, w_in, w_in_b, conv_w, conv_b, dt_bias, a_log, d_skip, ssm_norm_w,
           w_attn_b, w_ssm_b, w_out_b, gate, norm2):
    h2d = h.reshape(TOKENS, D_MODEL)
    qkv = _qkv_proj(h, w_in_b, l)
    w_gates = w_in_b[l:l + 1, :, OFF_GA:]
    proj = _in_proj(h2d, w_in_b, l, OFF_Z, D_INNER + CONV_DIM, w_gates)
    w_dt_t = w_in[l, :, OFF_DT:OFF_GA].T.astype(BF16)
    dtt = _dt_proj(h, w_dt_t, dt_bias[l])

    attn = _attention(qkv, rel_bias)

    cw, cbias = conv_w[l], conv_b[l]
    xs_w = cw[:, :D_INNER].reshape(CONV_K, SSM_GROUPS, GROUP_W)
    b_w = cw[:, D_INNER:D_INNER + SSM_GROUPS * D_STATE].reshape(CONV_K, SSM_GROUPS, D_STATE)
    c_w = cw[:, D_INNER + SSM_GROUPS * D_STATE:].reshape(CONV_K, SSM_GROUPS, D_STATE)
    conv_w_g = jnp.concatenate([xs_w, b_w, c_w], axis=2).transpose(1, 0, 2)
    conv_b_g = jnp.concatenate([cbias[:D_INNER].reshape(SSM_GROUPS, GROUP_W),
                                cbias[D_INNER:D_INNER + SSM_GROUPS * D_STATE].reshape(SSM_GROUPS, D_STATE),
                                cbias[D_INNER + SSM_GROUPS * D_STATE:].reshape(SSM_GROUPS, D_STATE)],
                               axis=1).reshape(SSM_GROUPS, 1, CONV_GROUP_W)
    a_col = jnp.broadcast_to(a_log[l].reshape(SSM_GROUPS, HEADS_PER_SSM_GROUP, 1),
                             (SSM_GROUPS, HEADS_PER_SSM_GROUP, LANES))
    dskip_x = jnp.repeat(d_skip[l], SSM_HEAD_DIM).reshape(SSM_GROUPS, 1, GROUP_W)
    y = _ssd(proj, dtt, conv_w_g, conv_b_g, a_col, dskip_x, ssm_norm_w[l])

    merged = _merge_proj(attn.reshape(TOKENS, ATTN_OUT), y.reshape(TOKENS, D_INNER), proj, w_attn_b, w_ssm_b, l)
    return _resid_norm_proj(merged, w_out_b, l, x, gate, *norm2)


def kernel(x, c, rel_bias, norm1_w, norm2_w, w_mod, b_mod, w_in, conv_w, conv_b, dt_bias, a_log, d_skip,
           ssm_norm_w, w_attn_proj, w_ssm_proj, w_out, w_ffn_in, w_ffn_out, final_norm_w):
    mod = _modulation(c, w_mod, b_mod)[:, :BATCH]
    xt = x.reshape(TOKENS, D_MODEL)
    w_in_b, w_attn_b, w_ssm_b, w_out_b, w_ffn_in_b, w_ffn_out_b = [
        w.astype(BF16) for w in (w_in, w_attn_proj, w_ssm_proj, w_out, w_ffn_in, w_ffn_out)]
    for l in range(DEPTH):
        sh1, sc1, g1, sh2, sc2, g2 = [m.reshape(BATCH, 1, D_MODEL) for m in jnp.split(mod[l], 6, axis=-1)]
        h = _norm(xt.reshape(BATCH, SEQ, D_MODEL), norm1_w[l], sc1, sh1, BF16, True)
        xt, h = _mixer(xt, h, l, rel_bias, w_in, w_in_b, conv_w, conv_b, dt_bias, a_log, d_skip, ssm_norm_w,
                       w_attn_b, w_ssm_b, w_out_b, g1, (norm2_w[l], sc2, sh2))
        u = _swiglu_in(h, w_ffn_in_b, l)
        xt = _resid_proj(u, w_ffn_out_b, l, xt, g2, "ffn_out_resid")
    zeros = jnp.zeros((BATCH, 1, D_MODEL), F32)
    return _norm(xt.reshape(BATCH, SEQ, D_MODEL), final_norm_w, zeros, zeros, F32, False)
```

```python
import functools
import math

import jax
import jax.numpy as jnp
import numpy as np
from jax import lax
from jax.experimental import pallas as pl
from jax.experimental.pallas import tpu as pltpu

F32 = jnp.float32
BF16 = jnp.bfloat16

D_MODEL = 2048
BATCH = 4
SEQ = 4096
TOKENS = BATCH * SEQ
DEPTH = 2
DILATED_GROUPS = ((128, 1), (512, 4), (2048, 16))
N_ATTN_GROUPS = len(DILATED_GROUPS)
HEADS_PER_GROUP = 4
N_ATTN_HEADS = N_ATTN_GROUPS * HEADS_PER_GROUP
HEAD_DIM = 128
QKV_W = N_ATTN_HEADS * HEAD_DIM
ATTN_OUT = HEADS_PER_GROUP * HEAD_DIM
ATTN_BLK = 128
REL_BUCKETS = 32
REL_MAX_DIST = 2048
D_INNER = 2 * D_MODEL
SSM_HEAD_DIM = 64
SSM_HEADS = D_INNER // SSM_HEAD_DIM
SSM_GROUPS = 8
HEADS_PER_SSM_GROUP = SSM_HEADS // SSM_GROUPS
GROUP_W = HEADS_PER_SSM_GROUP * SSM_HEAD_DIM
D_STATE = 128
CONV_K = 4
CHUNK = 128
CONV_DIM = D_INNER + 2 * SSM_GROUPS * D_STATE
CONV_GROUP_W = GROUP_W + 2 * D_STATE
D_FF = 5632
SPLITS = (QKV_W, QKV_W, QKV_W, D_INNER, CONV_DIM, SSM_HEADS, D_MODEL, D_MODEL)
OFF_Q, OFF_K, OFF_V, OFF_Z, OFF_XBC, OFF_DT, OFF_GA, OFF_GS = np.cumsum((0,) + SPLITS[:-1]).tolist()
N_QKV_HEADS = 3 * N_ATTN_HEADS
PROJ_W = D_INNER + CONV_DIM
P_Z = 0
P_XBC = D_INNER
P_B = P_XBC + D_INNER
P_C = P_B + SSM_GROUPS * D_STATE
EPS = 1e-6
LOG2_E = math.log2(math.e)

LANES = 128
VMEM_LIMIT_CAP = 60 * 1024 * 1024


def _nbytes(shape, dtype):
    return int(np.prod(shape)) * jnp.dtype(dtype).itemsize


def _params(sem, blocks, extra=0):
    need = 2 * sum(_nbytes(s, d) for s, d in blocks) + extra
    return pltpu.CompilerParams(dimension_semantics=sem,
                                vmem_limit_bytes=min(VMEM_LIMIT_CAP, need + (8 << 20)))


def _sigmoid(v):
    return 0.5 + 0.5 * jnp.tanh(0.5 * v)


def _silu(v):
    h = 0.5 * v
    return h + h * jnp.tanh(h)


MOD_TN = 1024
MOD_ROWS = 8


def _mod_kernel(c_ref, w_ref, b_ref, o_ref):
    ca = _silu(c_ref[...])
    o_ref[0] = jnp.dot(ca, w_ref[0], preferred_element_type=F32) + b_ref[0]


def _modulation(c, w_mod, b_mod):
    c8 = jnp.zeros((MOD_ROWS, D_MODEL), F32).at[:BATCH].set(c)
    n = 6 * D_MODEL
    blocks = [((MOD_ROWS, D_MODEL), F32), ((D_MODEL, MOD_TN), F32), ((1, MOD_TN), F32), ((MOD_ROWS, MOD_TN), F32)]
    return pl.pallas_call(
        _mod_kernel,
        grid=(DEPTH, n // MOD_TN),
        in_specs=[pl.BlockSpec((MOD_ROWS, D_MODEL), lambda l, j: (0, 0)),
                  pl.BlockSpec((1, D_MODEL, MOD_TN), lambda l, j: (l, 0, j)),
                  pl.BlockSpec((1, 1, MOD_TN), lambda l, j: (l, 0, j))],
        out_specs=pl.BlockSpec((1, MOD_ROWS, MOD_TN), lambda l, j: (l, 0, j)),
        out_shape=jax.ShapeDtypeStruct((DEPTH, MOD_ROWS, n), F32),
        compiler_params=_params(("arbitrary", "arbitrary"), blocks),
        name="modulation",
    )(c8, w_mod, b_mod.reshape(DEPTH, 1, n))


NORM_TS = 1024


def _norm_kernel(x_ref, w_ref, sc_ref, sh_ref, o_ref, *, modulate):
    x = x_ref[0]
    y = x * lax.rsqrt(jnp.mean(x * x, axis=-1, keepdims=True) + EPS)
    y = y * w_ref[...]
    if modulate:
        y = y * (1.0 + sc_ref[0]) + sh_ref[0]
    o_ref[0] = y.astype(o_ref.dtype)


def _norm(x, w, sc, sh, out_dtype, modulate):
    blocks = [((NORM_TS, D_MODEL), F32), ((NORM_TS, D_MODEL), out_dtype)]
    return pl.pallas_call(
        functools.partial(_norm_kernel, modulate=modulate),
        grid=(BATCH, SEQ // NORM_TS),
        in_specs=[pl.BlockSpec((1, NORM_TS, D_MODEL), lambda b, i: (b, i, 0)),
                  pl.BlockSpec((1, D_MODEL), lambda b, i: (0, 0)),
                  pl.BlockSpec((1, 1, D_MODEL), lambda b, i: (b, 0, 0)),
                  pl.BlockSpec((1, 1, D_MODEL), lambda b, i: (b, 0, 0))],
        out_specs=pl.BlockSpec((1, NORM_TS, D_MODEL), lambda b, i: (b, i, 0)),
        out_shape=jax.ShapeDtypeStruct((BATCH, SEQ, D_MODEL), out_dtype),
        compiler_params=_params(("arbitrary", "arbitrary"), blocks, extra=3 * _nbytes((NORM_TS, D_MODEL), F32)),
        name="rmsnorm_mod" if modulate else "rmsnorm",
    )(x, w.reshape(1, D_MODEL), sc, sh)


def _mm_kernel(a_ref, w_ref, o_ref):
    o_ref[...] = jnp.dot(a_ref[...], w_ref[0], preferred_element_type=F32).astype(o_ref.dtype)


def _matmul(a, w, layer, col0, n, tm, tn, name):
    m, k = a.shape
    blocks = [((tm, k), BF16), ((k, tn), BF16), ((tm, tn), BF16)]
    return pl.pallas_call(
        _mm_kernel,
        grid=(m // tm, n // tn),
        in_specs=[pl.BlockSpec((tm, k), lambda i, j: (i, 0)),
                  pl.BlockSpec((1, k, tn), lambda i, j: (layer, 0, col0 // tn + j))],
        out_specs=pl.BlockSpec((tm, tn), lambda i, j: (i, j)),
        out_shape=jax.ShapeDtypeStruct((m, n), BF16),
        compiler_params=_params(("arbitrary", "arbitrary"), blocks, extra=_nbytes((tm, tn), F32)),
        name=name,
    )(a, w)


QKV_TM = 2048
QKV_HEADS_PER_STEP = 6


def _qkv_kernel(h_ref, w_ref, o_ref):
    res = jnp.dot(h_ref[0], w_ref[0], preferred_element_type=F32)
    for hd in range(QKV_HEADS_PER_STEP):
        o_ref[0, hd] = res[:, hd * HEAD_DIM:(hd + 1) * HEAD_DIM]


def _qkv_proj(h, w, layer):
    tn = QKV_HEADS_PER_STEP * HEAD_DIM
    blocks = [((QKV_TM, D_MODEL), BF16), ((D_MODEL, tn), BF16), ((QKV_TM, tn), F32)]
    return pl.pallas_call(
        _qkv_kernel,
        grid=(BATCH, SEQ // QKV_TM, N_QKV_HEADS // QKV_HEADS_PER_STEP),
        in_specs=[pl.BlockSpec((1, QKV_TM, D_MODEL), lambda b, i, j: (b, i, 0)),
                  pl.BlockSpec((1, D_MODEL, tn), lambda b, i, j: (layer, 0, j))],
        out_specs=pl.BlockSpec((1, QKV_HEADS_PER_STEP, QKV_TM, HEAD_DIM), lambda b, i, j: (b, j, i, 0)),
        out_shape=jax.ShapeDtypeStruct((BATCH, N_QKV_HEADS, SEQ, HEAD_DIM), F32),
        compiler_params=_params(("arbitrary", "arbitrary", "arbitrary"), blocks, extra=_nbytes((QKV_TM, tn), F32)),
        name="qkv_proj",
    )(h, w)


DT_TS = 1024


def _dt_kernel(h_ref, wt_ref, bias_ref, o_ref):
    raw = lax.dot_general(wt_ref[...], h_ref[0], (((1,), (1,)), ((), ())), preferred_element_type=F32)
    v = raw + bias_ref[...]
    o_ref[0] = jnp.maximum(v, 0.0) + jnp.log1p(jnp.exp(-jnp.abs(v)))


def _dt_proj(h, w_dt_t, dt_bias):
    blocks = [((DT_TS, D_MODEL), BF16), ((SSM_HEADS, D_MODEL), BF16), ((SSM_HEADS, DT_TS), F32)]
    return pl.pallas_call(
        _dt_kernel,
        grid=(BATCH, SEQ // DT_TS),
        in_specs=[pl.BlockSpec((1, DT_TS, D_MODEL), lambda b, i: (b, i, 0)),
                  pl.BlockSpec((SSM_HEADS, D_MODEL), lambda b, i: (0, 0)),
                  pl.BlockSpec((SSM_HEADS, 1), lambda b, i: (0, 0))],
        out_specs=pl.BlockSpec((1, SSM_HEADS, DT_TS), lambda b, i: (b, 0, i)),
        out_shape=jax.ShapeDtypeStruct((BATCH, SSM_HEADS, SEQ), F32),
        compiler_params=_params(("arbitrary", "arbitrary"), blocks),
        name="dt_proj",
    )(h, w_dt_t, dt_bias.reshape(SSM_HEADS, 1))


ATTN_SPAN = 2048
SUB_BLOCKS = ATTN_SPAN // ATTN_BLK
MERGE_ROWS = 256


def _bucket_maps():
    qi = np.arange(ATTN_BLK)[:, None]
    kj = np.arange(2 * ATTN_BLK)[None, :]
    steps = np.clip(ATTN_BLK + qi - kj, 0, ATTN_BLK)
    exact = REL_BUCKETS // 2
    maps = []
    for _, dil in DILATED_GROUPS:
        dist = steps * dil
        n = np.maximum(dist, 1).astype(np.float32)
        large = exact + (np.log(n / np.float32(exact)) / np.float32(math.log(REL_MAX_DIST / exact))
                         * np.float32(REL_BUCKETS - exact)).astype(np.int32)
        large = np.minimum(large, REL_BUCKETS - 1)
        maps.append(np.where(dist < exact, dist, large).astype(np.int32))
    return np.stack(maps)


def _attn_kernel(tab_ref, bucket_ref, *refs):
    qkv_refs = refs[:5 * N_ATTN_GROUPS]
    out_ref, o_scr, l_scr, bias_ref, s_scr, p_scr, d_scr = refs[5 * N_ATTN_GROUPS:]
    b, hd, n = pl.program_id(0), pl.program_id(1), pl.program_id(2)

    kj2 = lax.broadcasted_iota(jnp.int32, (ATTN_BLK, 2 * ATTN_BLK), 1)
    qi2 = lax.broadcasted_iota(jnp.int32, (ATTN_BLK, 2 * ATTN_BLK), 0)

    @pl.when((b == 0) & (n == 0))
    def _():
        band = (kj2 >= qi2) & (kj2 <= qi2 + ATTN_BLK)
        for g in range(N_ATTN_GROUPS):
            bucket = bucket_ref[g]
            bias = jnp.zeros((ATTN_BLK, 2 * ATTN_BLK), F32)
            for k in range(REL_BUCKETS):
                bias = jnp.where(bucket == k, tab_ref[k, g * HEADS_PER_GROUP + hd], bias)
            bias_ref[g * HEADS_PER_GROUP + hd] = jnp.where(band, bias * LOG2_E, -jnp.inf)

    has_prev = (kj2 >= ATTN_BLK) | (n > 0)
    scale = LOG2_E / math.sqrt(HEAD_DIM)
    nt = (((1,), (1,)), ((), ()))

    for g, (_, dil) in enumerate(DILATED_GROUPS):
        q_ref, kc_ref, kp_ref, vc_ref, vp_ref = qkv_refs[5 * g:5 * g + 5]
        bias = bias_ref[g * HEADS_PER_GROUP + hd]
        bias_first = jnp.where(has_prev, bias, -jnp.inf)
        per_residue = SUB_BLOCKS // dil

        def rows(r, m, dil=dil):
            start = r + m * ATTN_BLK * dil
            return pl.ds(start, ATTN_BLK, stride=dil) if dil > 1 else pl.ds(start, ATTN_BLK)

        def both(prev_ref, cur_ref, r, m):
            prev = prev_ref[0, 0, rows(r, per_residue - 1), :] if m == 0 else cur_ref[0, 0, rows(r, m - 1), :]
            return jnp.concatenate([prev.astype(BF16), cur_ref[0, 0, rows(r, m), :].astype(BF16)], axis=0)

        subs = [(r, m) for r in range(dil) for m in range(per_residue)]
        for i, (r, m) in enumerate(subs):
            q = q_ref[0, 0, rows(r, m), :].astype(BF16)
            s = lax.dot_general(q, both(kp_ref, kc_ref, r, m), nt, preferred_element_type=F32)
            s_scr[i] = s * scale + (bias_first if m == 0 else bias)
        for i, (r, m) in enumerate(subs):
            s = s_scr[i]
            mx = jnp.max(s, axis=-1, keepdims=True)
            p = jnp.exp2(s - mx)
            den = jnp.sum(p, axis=-1, keepdims=True)
            p_scr[i] = p.astype(BF16)
            d_scr[i] = jnp.broadcast_to(1.0 / den, (ATTN_BLK, HEAD_DIM))
            l_scr[g, rows(r, m), :] = jnp.broadcast_to(mx + jnp.log2(den), (ATTN_BLK, HEAD_DIM))
        for i, (r, m) in enumerate(subs):
            acc = jnp.dot(p_scr[i], both(vp_ref, vc_ref, r, m), preferred_element_type=F32)
            o_scr[g, rows(r, m), :] = acc * d_scr[i]

    for c in range(ATTN_SPAN // MERGE_ROWS):
        sl = pl.ds(c * MERGE_ROWS, MERGE_ROWS)
        l0, l1, l2 = l_scr[0, sl, :], l_scr[1, sl, :], l_scr[2, sl, :]
        mx = jnp.maximum(jnp.maximum(l0, l1), l2)
        e0, e1, e2 = jnp.exp2(l0 - mx), jnp.exp2(l1 - mx), jnp.exp2(l2 - mx)
        attn = (e0 * o_scr[0, sl, :] + e1 * o_scr[1, sl, :] + e2 * o_scr[2, sl, :]) / (e0 + e1 + e2)
        out_ref[0, sl, :] = attn.astype(out_ref.dtype)


def _attention(qkv, rel_bias):
    blk = (1, 1, ATTN_SPAN, HEAD_DIM)

    def spec(which, g, prev):
        base = which * N_ATTN_HEADS + g * HEADS_PER_GROUP
        if prev:
            return pl.BlockSpec(blk, lambda b, hd, n: (b, base + hd, jnp.maximum(n - 1, 0), 0))
        return pl.BlockSpec(blk, lambda b, hd, n: (b, base + hd, n, 0))

    qkv_specs = []
    for g in range(N_ATTN_GROUPS):
        qkv_specs += [spec(0, g, False), spec(1, g, False), spec(1, g, True), spec(2, g, False), spec(2, g, True)]
    bias_shape = (N_ATTN_HEADS, ATTN_BLK, 2 * ATTN_BLK)
    scr_shape = (N_ATTN_GROUPS, ATTN_SPAN, HEAD_DIM)
    stage_shape = (SUB_BLOCKS, ATTN_BLK, 2 * ATTN_BLK)
    blocks =[(blk, F32)] * len(qkv_specs) + [((ATTN_SPAN, HEAD_DIM), BF16), (bias_shape, jnp.int32)]
    return pl.pallas_call(
        _attn_kernel,
        grid=(BATCH, HEADS_PER_GROUP, SEQ // ATTN_SPAN),
        in_specs=[pl.BlockSpec(memory_space=pltpu.SMEM),
                  pl.BlockSpec((N_ATTN_GROUPS, ATTN_BLK, 2 * ATTN_BLK), lambda b, hd, n: (0, 0, 0))] + qkv_specs,
        out_specs=pl.BlockSpec((1, ATTN_SPAN, HEAD_DIM), lambda b, hd, n: (b, n, hd)),
        out_shape=jax.ShapeDtypeStruct((BATCH, SEQ, ATTN_OUT), BF16),
        scratch_shapes=[pltpu.VMEM(scr_shape, F32), pltpu.VMEM(scr_shape, F32), pltpu.VMEM(bias_shape, F32),
                        pltpu.VMEM(stage_shape, F32), pltpu.VMEM(stage_shape, BF16),
                        pltpu.VMEM((SUB_BLOCKS, ATTN_BLK, HEAD_DIM), F32)],
        compiler_params=_params(("arbitrary", "arbitrary", "arbitrary"), blocks,
                                extra=2 * _nbytes(scr_shape, F32) + _nbytes(bias_shape, F32)
                                + _nbytes(stage_shape, F32) + _nbytes(stage_shape, BF16)
                                + _nbytes((SUB_BLOCKS, ATTN_BLK, HEAD_DIM), F32)),
        name="dilated_attn",
    )(rel_bias, jnp.asarray(_bucket_maps()), *([qkv] * len(qkv_specs)))


N_CHUNKS = SEQ // CHUNK
HEAD_PAIRS = HEADS_PER_SSM_GROUP // 2


def _split3(v):
    hi = v.astype(BF16)
    r1 = v - hi.astype(F32)
    mid = r1.astype(BF16)
    lo = (r1 - mid.astype(F32)).astype(BF16)
    return hi, mid, lo


def _ssd_kernel(xs_ref, bm_ref, cm_ref, z_ref, dtt_ref, cw_ref, cb_ref, acol_ref, dskip_ref, nw_ref,
                y_ref, state_ref):
    state_ref[...] = jnp.zeros_like(state_ref)

    li = lax.broadcasted_iota(jnp.int32, (CHUNK, CHUNK), 0)
    si = lax.broadcasted_iota(jnp.int32, (CHUNK, CHUNK), 1)
    tril = li >= si
    ltri = jnp.where(tril, 1.0, 0.0).astype(BF16)
    utri = jnp.where(li <= si, 1.0, 0.0).astype(BF16)
    low_half = si < SSM_HEAD_DIM
    low_row = lax.broadcasted_iota(jnp.int32, (1, LANES), 1) < SSM_HEAD_DIM
    nt = (((1,), (1,)), ((), ()))
    ri = lax.broadcasted_iota(jnp.int32, (CONV_K * CHUNK, 2 * CHUNK), 0)
    ci = lax.broadcasted_iota(jnp.int32, (CONV_K * CHUNK, 2 * CHUNK), 1)
    shift = jnp.where(ci == CHUNK + (ri & (CHUNK - 1)) - (ri >> 7), 1.0, 0.0).astype(BF16)

    cw = cw_ref[0]
    cb = cb_ref[0]
    a_col = -jnp.exp(acol_ref[0]) * LOG2_E
    dskip = dskip_ref[0]
    nw = nw_ref[...]

    def load(rows):
        return jnp.concatenate([xs_ref[0, rows, :], bm_ref[0, rows, :], cm_ref[0, rows, :]], axis=1)

    def chunk(c, carry):
        r0 = pl.multiple_of(c * CHUNK, CHUNK)
        rows = pl.ds(r0, CHUNK)
        u_prev = load(pl.ds(pl.multiple_of(jnp.maximum(r0 - CHUNK, 0), CHUNK), CHUNK))
        u_prev = jnp.where(c > 0, u_prev, jnp.zeros_like(u_prev))
        taps = jnp.dot(shift, jnp.concatenate([u_prev, load(rows)], axis=0), preferred_element_type=F32)
        acc = cb + cw[CONV_K - 1:CONV_K, :] * taps[0:CHUNK]
        for s in range(1, CONV_K):
            acc = acc + cw[CONV_K - 1 - s:CONV_K - s, :] * taps[s * CHUNK:(s + 1) * CHUNK]
        xbc = _silu(acc)
        x = xbc[:, :GROUP_W]
        bmat = xbc[:, GROUP_W:GROUP_W + D_STATE]
        cmat = xbc[:, GROUP_W + D_STATE:]

        dtt = dtt_ref[0, :, rows]
        dat3 = _split3(dtt * a_col)
        acum_t = sum(jnp.dot(p, utri, preferred_element_type=F32) for p in dat3)
        acum = sum(lax.dot_general(ltri, p, nt, preferred_element_type=F32) for p in dat3)
        w_t = dtt * jnp.exp2(acum_t[:, CHUNK - 1:CHUNK] - acum_t)
        cdec = jnp.exp2(acum[CHUNK - 1:CHUNK, :])

        bt = bmat.T
        cbm = jnp.dot(cmat.astype(BF16), bt.astype(BF16), preferred_element_type=F32)
        x_b = x.astype(BF16)

        y_blocks = []
        for k in range(HEAD_PAIRS):
            cols = slice(k * LANES, (k + 1) * LANES)
            xk = x_b[:, cols]
            zero = jnp.zeros_like(xk)
            x_lo, x_hi = jnp.where(low_half, xk, zero), jnp.where(low_half, zero, xk)
            st = state_ref[:, cols]
            st_b = st.astype(BF16)
            st_lo, st_hi = jnp.where(low_half, st_b, zero), jnp.where(low_half, zero, st_b)
            ms, cs, bts, cds = [], [], [], []
            for r in (2 * k, 2 * k + 1):
                col = jnp.broadcast_to(acum[:, r:r + 1], (CHUNK, CHUNK))
                row = jnp.broadcast_to(acum_t[r:r + 1, :], (CHUNK, CHUNK))
                ldec = jnp.exp2(jnp.where(tril, col - row, -jnp.inf))
                ms.append((cbm * ldec * jnp.broadcast_to(dtt[r:r + 1, :], (CHUNK, CHUNK))).astype(BF16))
                cs.append((cmat * jnp.exp2(col)).astype(BF16))
                bts.append((bt * jnp.broadcast_to(w_t[r:r + 1, :], (CHUNK, CHUNK))).astype(BF16))
                cds.append(jnp.broadcast_to(cdec[:, r:r + 1], (1, LANES)))
            y_blocks.append(jnp.dot(jnp.concatenate(ms + cs, axis=1),
                                    jnp.concatenate([x_lo, x_hi, st_lo, st_hi], axis=0),
                                    preferred_element_type=F32))
            upd = jnp.dot(jnp.concatenate(bts, axis=1), jnp.concatenate([x_lo, x_hi], axis=0),
                          preferred_element_type=F32)
            state_ref[:, cols] = st * jnp.where(low_row, cds[0], cds[1]) + upd
        y = jnp.concatenate(y_blocks, axis=1)

        y = y + dskip * x
        y = y * _silu(z_ref[0, rows, :].astype(F32))
        y = y * lax.rsqrt(jnp.mean(y * y, axis=-1, keepdims=True) + EPS)
        y_ref[0, rows, :] = (y * nw).astype(y_ref.dtype)
        return carry

    lax.fori_loop(0, N_CHUNKS, chunk, 0, unroll=8)


def _ssd(proj, dtt, conv_w_g, conv_b_g, a_col, dskip_x, norm_w):
    view = proj.reshape(BATCH, SEQ, PROJ_W)
    gw, ns = GROUP_W, D_STATE
    blocks = [((SEQ, gw), BF16)] * 3 + [((SEQ, ns), BF16)] * 2 + [((HEADS_PER_SSM_GROUP, SEQ), F32)]
    return pl.pallas_call(
        _ssd_kernel,
        grid=(BATCH, SSM_GROUPS),
        in_specs=[pl.BlockSpec((1, SEQ, gw), lambda b, g: (b, 0, P_XBC // gw + g)),
                  pl.BlockSpec((1, SEQ, ns), lambda b, g: (b, 0, P_B // ns + g)),
                  pl.BlockSpec((1, SEQ, ns), lambda b, g: (b, 0, P_C // ns + g)),
                  pl.BlockSpec((1, SEQ, gw), lambda b, g: (b, 0, P_Z // gw + g)),
                  pl.BlockSpec((1, HEADS_PER_SSM_GROUP, SEQ), lambda b, g: (b, g, 0)),
                  pl.BlockSpec((1, CONV_K, CONV_GROUP_W), lambda b, g: (g, 0, 0)),
                  pl.BlockSpec((1, 1, CONV_GROUP_W), lambda b, g: (g, 0, 0)),
                  pl.BlockSpec((1, HEADS_PER_SSM_GROUP, LANES), lambda b, g: (g, 0, 0)),
                  pl.BlockSpec((1, 1, gw), lambda b, g: (g, 0, 0)),
                  pl.BlockSpec((1, gw), lambda b, g: (0, g))],
        out_specs=pl.BlockSpec((1, SEQ, gw), lambda b, g: (b, 0, g)),
        out_shape=jax.ShapeDtypeStruct((BATCH, SEQ, D_INNER), BF16),
        scratch_shapes=[pltpu.VMEM((ns, gw), F32)],
        compiler_params=_params(("arbitrary", "arbitrary"), blocks, extra=_nbytes((ns, gw), F32) + (8 << 20)),
        name="ssd_scan",
    )(view, view, view, view, dtt, conv_w_g, conv_b_g, a_col, dskip_x, norm_w.reshape(1, D_INNER))


MG_TM = 1024
MG_TN = 512


def _merge_kernel(attn_ref, y_ref, ga_ref, gs_ref, wa_ref, ws_ref, out_ref):
    pa = jnp.dot(attn_ref[...], wa_ref[0], preferred_element_type=F32)
    ps = jnp.dot(y_ref[...], ws_ref[0], preferred_element_type=F32)
    merged = _sigmoid(ga_ref[...].astype(F32)) * pa + _sigmoid(gs_ref[...].astype(F32)) * ps
    out_ref[...] = merged.astype(out_ref.dtype)


def _merge_proj(attn, y, gates, w_attn, w_ssm, layer):
    row = lambda i, j: (i, 0)
    blocks = [((MG_TM, ATTN_OUT), BF16), ((MG_TM, D_INNER), BF16)] + [((MG_TM, MG_TN), BF16)] * 3 \
        + [((ATTN_OUT, MG_TN), BF16), ((D_INNER, MG_TN), BF16)]
    return pl.pallas_call(
        _merge_kernel,
        grid=(TOKENS // MG_TM, D_MODEL // MG_TN),
        in_specs=[pl.BlockSpec((MG_TM, ATTN_OUT), row),
                  pl.BlockSpec((MG_TM, D_INNER), row),
                  pl.BlockSpec((MG_TM, MG_TN), lambda i, j: (i, j)),
                  pl.BlockSpec((MG_TM, MG_TN), lambda i, j: (i, D_MODEL // MG_TN + j)),
                  pl.BlockSpec((1, ATTN_OUT, MG_TN), lambda i, j: (layer, 0, j)),
                  pl.BlockSpec((1, D_INNER, MG_TN), lambda i, j: (layer, 0, j))],
        out_specs=pl.BlockSpec((MG_TM, MG_TN), lambda i, j: (i, j)),
        out_shape=jax.ShapeDtypeStruct((TOKENS, D_MODEL), BF16),
        compiler_params=_params(("arbitrary", "arbitrary"), blocks, extra=4 * _nbytes((MG_TM, MG_TN), F32)),
        name="merge_proj",
    )(attn, y, gates, gates, w_attn, w_ssm)


def _resid_kernel(a_ref, w_ref, x_ref, g_ref, o_ref):
    o_ref[...] = x_ref[...] + g_ref[0] * jnp.dot(a_ref[...], w_ref[0], preferred_element_type=F32)


def _resid_proj(a, w, layer, x, gate, name):
    k = a.shape[1]
    tm, tn = 1024, 512
    blocks = [((tm, k), BF16), ((k, tn), BF16), ((tm, tn), F32), ((tm, tn), F32)]
    return pl.pallas_call(
        _resid_kernel,
        grid=(TOKENS // tm, D_MODEL // tn),
        in_specs=[pl.BlockSpec((tm, k), lambda i, j: (i, 0)),
                  pl.BlockSpec((1, k, tn), lambda i, j: (layer, 0, j)),
                  pl.BlockSpec((tm, tn), lambda i, j: (i, j)),
                  pl.BlockSpec((1, 1, tn), lambda i, j: (i * tm // SEQ, 0, j))],
        out_specs=pl.BlockSpec((tm, tn), lambda i, j: (i, j)),
        out_shape=jax.ShapeDtypeStruct((TOKENS, D_MODEL), F32),
        compiler_params=_params(("arbitrary", "arbitrary"), blocks, extra=_nbytes((tm, tn), F32)),
        name=name,
    )(a, w, x, gate)


RN_TM = 512


def _resid_norm_kernel(a_ref, w_ref, x_ref, g_ref, nw_ref, sc_ref, sh_ref, o_ref, h_ref):
    x1 = x_ref[...] + g_ref[0] * jnp.dot(a_ref[...], w_ref[0], preferred_element_type=F32)
    o_ref[...] = x1
    y = x1 * lax.rsqrt(jnp.mean(x1 * x1, axis=-1, keepdims=True) + EPS)
    y = y * nw_ref[...]
    h_ref[...] = (y * (1.0 + sc_ref[0]) + sh_ref[0]).astype(h_ref.dtype)


def _resid_norm_proj(a, w, layer, x, gate, norm_w, sc, sh):
    tm = RN_TM
    row = lambda i: (i, 0)
    per_batch = lambda i: (i * tm // SEQ, 0, 0)
    blocks = [((tm, D_MODEL), BF16), ((D_MODEL, D_MODEL), BF16), ((tm, D_MODEL), F32), ((tm, D_MODEL), F32),
              ((tm, D_MODEL), BF16)]
    return pl.pallas_call(
        _resid_norm_kernel,
        grid=(TOKENS // tm,),
        in_specs=[pl.BlockSpec((tm, D_MODEL), row),
                  pl.BlockSpec((1, D_MODEL, D_MODEL), lambda i: (layer, 0, 0)),
                  pl.BlockSpec((tm, D_MODEL), row),
                  pl.BlockSpec((1, 1, D_MODEL), per_batch),
                  pl.BlockSpec((1, D_MODEL), lambda i: (0, 0)),
                  pl.BlockSpec((1, 1, D_MODEL), per_batch),
                  pl.BlockSpec((1, 1, D_MODEL), per_batch)],
        out_specs=[pl.BlockSpec((tm, D_MODEL), row), pl.BlockSpec((tm, D_MODEL), row)],
        out_shape=[jax.ShapeDtypeStruct((TOKENS, D_MODEL), F32), jax.ShapeDtypeStruct((TOKENS, D_MODEL), BF16)],
        compiler_params=_params(("arbitrary",), blocks, extra=2 * _nbytes((tm, D_MODEL), F32)),
        name="out_proj_resid_norm",
    )(a, w, x, gate, norm_w.reshape(1, D_MODEL), sc, sh)


FF_TM = 1024
FF_TN = 512


def _swiglu_kernel(h_ref, wg_ref, wu_ref, o_ref):
    h = h_ref[...]
    hg = jnp.dot(h, wg_ref[0], preferred_element_type=F32)
    hu = jnp.dot(h, wu_ref[0], preferred_element_type=F32)
    o_ref[...] = (_silu(hg) * hu).astype(o_ref.dtype)


def _swiglu_in(h, w, layer):
    nblk = D_FF // FF_TN
    blocks = [((FF_TM, D_MODEL), BF16), ((D_MODEL, FF_TN), BF16), ((D_MODEL, FF_TN), BF16), ((FF_TM, FF_TN), BF16)]
    return pl.pallas_call(
        _swiglu_kernel,
        grid=(TOKENS // FF_TM, nblk),
        in_specs=[pl.BlockSpec((FF_TM, D_MODEL), lambda i, j: (i, 0)),
                  pl.BlockSpec((1, D_MODEL, FF_TN), lambda i, j: (layer, 0, j)),
                  pl.BlockSpec((1, D_MODEL, FF_TN), lambda i, j: (layer, 0, nblk + j))],
        out_specs=pl.BlockSpec((FF_TM, FF_TN), lambda i, j: (i, j)),
        out_shape=jax.ShapeDtypeStruct((TOKENS, D_FF), BF16),
        compiler_params=_params(("arbitrary", "arbitrary"), blocks, extra=3 * _nbytes((FF_TM, FF_TN), F32)),
        name="swiglu_in",
    )(h, w, w)


def _mixer(x, h, l, rel_bias, w_in, w_in_b, conv_w, conv_b, dt_bias, a_log, d_skip, ssm_norm_w,
           w_attn_b, w_ssm_b, w_out_b, gate, norm2):
    h2d = h.reshape(TOKENS, D_MODEL)
    qkv = _qkv_proj(h, w_in_b, l)
    proj = _matmul(h2d, w_in_b, l, OFF_Z, PROJ_W, 2048, 512, "in_proj")
    w_gates = w_in_b[l:l + 1, :, OFF_GA:]
    gates = _matmul(h2d, w_gates, 0, 0, 2 * D_MODEL, 2048, 512, "gate_proj")
    w_dt_t = w_in[l, :, OFF_DT:OFF_GA].T.astype(BF16)
    dtt = _dt_proj(h, w_dt_t, dt_bias[l])

    attn = _attention(qkv, rel_bias)

    cw, cbias = conv_w[l], conv_b[l]
    xs_w = cw[:, :D_INNER].reshape(CONV_K, SSM_GROUPS, GROUP_W)
    b_w = cw[:, D_INNER:D_INNER + SSM_GROUPS * D_STATE].reshape(CONV_K, SSM_GROUPS, D_STATE)
    c_w = cw[:, D_INNER + SSM_GROUPS * D_STATE:].reshape(CONV_K, SSM_GROUPS, D_STATE)
    conv_w_g = jnp.concatenate([xs_w, b_w, c_w], axis=2).transpose(1, 0, 2)
    conv_b_g = jnp.concatenate([cbias[:D_INNER].reshape(SSM_GROUPS, GROUP_W),
                                cbias[D_INNER:D_INNER + SSM_GROUPS * D_STATE].reshape(SSM_GROUPS, D_STATE),
                                cbias[D_INNER + SSM_GROUPS * D_STATE:].reshape(SSM_GROUPS, D_STATE)],
                               axis=1).reshape(SSM_GROUPS, 1, CONV_GROUP_W)
    a_col = jnp.broadcast_to(a_log[l].reshape(SSM_GROUPS, HEADS_PER_SSM_GROUP, 1),
                             (SSM_GROUPS, HEADS_PER_SSM_GROUP, LANES))
    dskip_x = jnp.repeat(d_skip[l], SSM_HEAD_DIM).reshape(SSM_GROUPS, 1, GROUP_W)
    y = _ssd(proj, dtt, conv_w_g, conv_b_g, a_col, dskip_x, ssm_norm_w[l])

    merged = _merge_proj(attn.reshape(TOKENS, ATTN_OUT), y.reshape(TOKENS, D_INNER), gates, w_attn_b, w_ssm_b, l)
    return _resid_norm_proj(merged, w_out_b, l, x, gate, *norm2)


def kernel(x, c, rel_bias, norm1_w, norm2_w, w_mod, b_mod, w_in, conv_w, conv_b, dt_bias, a_log, d_skip,
           ssm_norm_w, w_attn_proj, w_ssm_proj, w_out, w_ffn_in, w_ffn_out, final_norm_w):
    mod = _modulation(c, w_mod, b_mod)[:, :BATCH]
    xt = x.reshape(TOKENS, D_MODEL)
    w_in_b, w_attn_b, w_ssm_b, w_out_b, w_ffn_in_b, w_ffn_out_b = [
        w.astype(BF16) for w in (w_in, w_attn_proj, w_ssm_proj, w_out, w_ffn_in, w_ffn_out)]
    for l in range(DEPTH):
        sh1, sc1, g1, sh2, sc2, g2 = [m.reshape(BATCH, 1, D_MODEL) for m in jnp.split(mod[l], 6, axis=-1)]
        h = _norm(xt.reshape(BATCH, SEQ, D_MODEL), norm1_w[l], sc1, sh1, BF16, True)
        xt, h = _mixer(xt, h, l, rel_bias, w_in, w_in_b, conv_w, conv_b, dt_bias, a_log, d_skip, ssm_norm_w,
                       w_attn_b, w_ssm_b, w_out_b, g1, (norm2_w[l], sc2, sh2))
        u = _swiglu_in(h, w_ffn_in_b, l)
        xt = _resid_proj(u, w_ffn_out_b, l, xt, g2, "ffn_out_resid")
    zeros = jnp.zeros((BATCH, 1, D_MODEL), F32)
    return _norm(xt.reshape(BATCH, SEQ, D_MODEL), final_norm_w, zeros, zeros, F32, False)
```
